```python
import math
import jax, jax.numpy as jnp
from jax import lax
import numpy as np

D_MODEL = 1024
BATCH = 32
SEQ = 2048
DEPTH = 4

HEAD_DIM = 64
A_GROUPS = 12
A_WIDTH = A_GROUPS * HEAD_DIM
A_CHUNK = 128
B_BLOCKS = 12
B_WIDTH = 768
B_BLOCK_DIM = B_WIDTH // B_BLOCKS
B_CONV = 4
RG_C = 8.0
C_HEADS = 12
C_WIDTH = C_HEADS * HEAD_DIM
C_CONFIGS = ((128, 1), (512, 4), (2048, 16))
ATT_BLOCK = 128
N_BUCKETS = 32
MAX_DISTANCE = 2048
N_BRANCH = 3
D_FF = 2816
FFN_CONV = 3
EPS = 1e-6
NEG_INF = -1e30
IN_COLS = 2 * A_WIDTH + 2 * B_WIDTH + 3 * C_WIDTH + N_BRANCH * D_MODEL

kernel_name = "hybrid_gated_parallel_mixers"


def _rmsnorm(x, g):
    x32 = x.astype(jnp.float32)
    y = x32 * lax.rsqrt(jnp.mean(x32 * x32, axis=-1, keepdims=True) + EPS)
    return (y * g.astype(jnp.float32)).astype(x.dtype)


def _layernorm(x, g):
    x32 = x.astype(jnp.float32)
    mu = jnp.mean(x32, axis=-1, keepdims=True)
    var = jnp.mean(jnp.square(x32 - mu), axis=-1, keepdims=True)
    return ((x32 - mu) * lax.rsqrt(var + EPS) * g.astype(jnp.float32)).astype(x.dtype)


def _modulate(x, shift, scale):
    return x * (1.0 + scale[:, None, :]) + shift[:, None, :]


def _causal_dwconv(x, w, b):
    k, c = w.shape
    y = lax.conv_general_dilated(
        x, w[:, None, :].astype(x.dtype), window_strides=(1,), padding=[(k - 1, 0)],
        dimension_numbers=("NWC", "WIO", "NWC"), feature_group_count=c)
    return y + b.astype(x.dtype)


def _t5_bucket(dist):
    max_exact = N_BUCKETS // 2
    d = np.maximum(dist, 1).astype(np.float32)
    large = max_exact + (np.log(d / max_exact) / np.log(MAX_DISTANCE / max_exact)
                         * (N_BUCKETS - max_exact)).astype(np.int32)
    large = np.minimum(large, N_BUCKETS - 1)
    return np.where(dist < max_exact, dist, large).astype(np.int32)


def _mixer_a(u, v, ln_g, w_s, b_s):
    bsz, s, _ = v.shape
    u = jax.nn.gelu(u)
    v = _layernorm(jax.nn.gelu(v), ln_g)
    vr = v.reshape(bsz, s // A_CHUNK, A_CHUNK, A_GROUPS, HEAD_DIM)
    mask = jnp.tril(jnp.ones((A_CHUNK, A_CHUNK), v.dtype))
    sv = jnp.einsum('gts,bnsgc->bntgc', w_s * mask, vr) + b_s.T[None, None, :, :, None]
    return u * sv.reshape(bsz, s, A_WIDTH)


def _lru_combine(left, right):
    a1, b1 = left
    a2, b2 = right
    return a1 * a2, a2 * b1 + b2


def _mixer_b(xb, gb, conv_w, conv_b, wa, ba, wx, bx, lam):
    bsz, s, _ = xb.shape
    xb = _causal_dwconv(xb, conv_w, conv_b)
    xr = xb.reshape(bsz, s, B_BLOCKS, B_BLOCK_DIM)
    r = jax.nn.sigmoid(jnp.einsum('bshi,hij->bshj', xr, wa).reshape(bsz, s, B_WIDTH) + ba)
    i = jax.nn.sigmoid(jnp.einsum('bshi,hij->bshj', xr, wx).reshape(bsz, s, B_WIDTH) + bx)
    log_a = -RG_C * r.astype(jnp.float32) * jax.nn.softplus(-lam.astype(jnp.float32))
    a = jnp.exp(log_a)
    mult = jnp.sqrt(-jnp.expm1(2.0 * log_a))
    mult = mult.at[:, 0].set(1.0)
    xin = xb.astype(jnp.float32) * i.astype(jnp.float32) * mult
    _, h = lax.associative_scan(_lru_combine, (a, xin), axis=1)
    return h.astype(xb.dtype) * jax.nn.gelu(gb)


def _dilated_attn(q, k, v, window, dil, rel_table):
    bsz, s, h, e = q.shape
    L = s // dil
    nb = -(-L // ATT_BLOCK)
    lp = nb * ATT_BLOCK
    nw = window // dil

    def to_blocks(t):
        t = t.reshape(bsz, L, dil, h, e)
        t = jnp.pad(t, ((0, 0), (0, lp - L), (0, 0), (0, 0), (0, 0)))
        return t.reshape(bsz, nb, ATT_BLOCK, dil, h, e)

    qb, kb, vb = to_blocks(q), to_blocks(k), to_blocks(v)

    def with_prev(t):
        prev = jnp.concatenate([jnp.zeros_like(t[:, :1]), t[:, :-1]], axis=1)
        return jnp.concatenate([prev, t], axis=2)

    kc, vc = with_prev(kb), with_prev(vb)
    qi = np.arange(ATT_BLOCK)[:, None]
    kk = np.arange(2 * ATT_BLOCK)[None, :]
    dist = qi + ATT_BLOCK - kk
    band = (dist >= 0) & (dist <= nw)
    blk = np.arange(nb)[:, None, None]
    valid = band[None] & ((blk > 0) | (kk[None] >= ATT_BLOCK))
    bucket = _t5_bucket(np.maximum(dist, 0) * dil)
    bias = jnp.transpose(rel_table[jnp.asarray(bucket)], (2, 0, 1)).astype(jnp.float32)

    logits = jnp.einsum('bnidhe,bnkdhe->bndhik', qb, kc).astype(jnp.float32) + bias[None, None, None]
    logits = jnp.where(jnp.asarray(valid)[None, :, None, None], logits, NEG_INF)
    m = jnp.max(logits, axis=-1, keepdims=True)
    p = jnp.exp(logits - m)
    den = jnp.sum(p, axis=-1)
    o = jnp.einsum('bndhik,bnkdhe->bnidhe', p, vc.astype(jnp.float32))
    o = o / jnp.transpose(den, (0, 1, 4, 2, 3))[..., None]
    lse = jnp.transpose(m[..., 0] + jnp.log(den), (0, 1, 4, 2, 3))
    o = o.reshape(bsz, lp, dil, h, e)[:, :L].reshape(bsz, s, h, e)
    lse = lse.reshape(bsz, lp, dil, h)[:, :L].reshape(bsz, s, h)
    return o, lse


def _mixer_c(q, k, v, rel_table):
    bsz, s, _ = q.shape
    q = q.reshape(bsz, s, C_HEADS, HEAD_DIM) * (HEAD_DIM ** -0.5)
    k = k.reshape(bsz, s, C_HEADS, HEAD_DIM)
    v = v.reshape(bsz, s, C_HEADS, HEAD_DIM)
    outs, lses = [], []
    for window, dil in C_CONFIGS:
        o, l = _dilated_attn(q, k, v, window, dil, rel_table)
        outs.append(o)
        lses.append(l)
    wts = jax.nn.softmax(jnp.stack(lses, axis=0), axis=0)
    o = jnp.einsum('nbsh,nbshe->bshe', wts, jnp.stack(outs, axis=0))
    return o.reshape(bsz, s, C_WIDTH).astype(q.dtype)


def _ffn(x, w_gate, w_up, conv_w, conv_b, w_down):
    g = _causal_dwconv(x @ w_gate, conv_w, conv_b)
    return (jax.nn.gelu(g) * (x @ w_up)) @ w_down


def setup_inputs(seed: int = 0) -> dict:
    key = jax.random.key(seed)
    ks = jax.random.split(key, 32)
    f32 = jnp.float32

    def nrm(k, shape, fan_in):
        return jax.random.normal(k, shape, f32) * (fan_in ** -0.5)

    def small(k, shape, s=0.02):
        return jax.random.normal(k, shape, f32) * s

    a8 = jax.random.uniform(ks[15], (DEPTH, B_WIDTH), f32, 0.9, 0.999)
    a_base = a8 ** (1.0 / RG_C)
    lam = jnp.log(a_base) - jnp.log1p(-a_base)
    return {
        "x": jax.random.normal(ks[0], (BATCH, SEQ, D_MODEL), f32),
        "c": jax.random.normal(ks[1], (BATCH, D_MODEL), f32),
        "w_ada": nrm(ks[2], (DEPTH, D_MODEL, 6 * D_MODEL), D_MODEL) * 0.5,
        "b_ada": small(ks[3], (DEPTH, 6 * D_MODEL)),
        "norm1": 1.0 + small(ks[4], (DEPTH, D_MODEL)),
        "w_in": nrm(ks[5], (DEPTH, D_MODEL, IN_COLS), D_MODEL),
        "a_ln": 1.0 + small(ks[6], (DEPTH, A_WIDTH)),
        "a_ws": nrm(ks[7], (DEPTH, A_GROUPS, A_CHUNK, A_CHUNK), A_CHUNK),
        "a_bs": small(ks[8], (DEPTH, A_GROUPS, A_CHUNK), 0.1),
        "b_conv_w": nrm(ks[9], (DEPTH, B_CONV, B_WIDTH), B_CONV),
        "b_conv_b": small(ks[10], (DEPTH, B_WIDTH)),
        "b_wa": nrm(ks[11], (DEPTH, B_BLOCKS, B_BLOCK_DIM, B_BLOCK_DIM), B_BLOCK_DIM),
        "b_ba": small(ks[12], (DEPTH, B_WIDTH)),
        "b_wx": nrm(ks[13], (DEPTH, B_BLOCKS, B_BLOCK_DIM, B_BLOCK_DIM), B_BLOCK_DIM),
        "b_bx": small(ks[14], (DEPTH, B_WIDTH)),
        "b_lam": lam,
        "rel_table": small(ks[16], (N_BUCKETS, C_HEADS), 0.5),
        "p_a": nrm(ks[17], (DEPTH, A_WIDTH, D_MODEL), A_WIDTH),
        "p_b": nrm(ks[18], (DEPTH, B_WIDTH, D_MODEL), B_WIDTH),
        "p_c": nrm(ks[19], (DEPTH, C_WIDTH, D_MODEL), C_WIDTH),
        "w_out": nrm(ks[20], (DEPTH, D_MODEL, D_MODEL), D_MODEL),
        "norm2": 1.0 + small(ks[21], (DEPTH, D_MODEL)),
        "f_wgate": nrm(ks[22], (DEPTH, D_MODEL, D_FF), D_MODEL),
        "f_wup": nrm(ks[23], (DEPTH, D_MODEL, D_FF), D_MODEL),
        "f_conv_w": nrm(ks[24], (DEPTH, FFN_CONV, D_FF), FFN_CONV),
        "f_conv_b": small(ks[25], (DEPTH, D_FF)),
        "f_wdown": nrm(ks[26], (DEPTH, D_FF, D_MODEL), D_FF),
        "final_norm": 1.0 + small(ks[27], (D_MODEL,)),
    }


def reference(x, c, w_ada, b_ada, norm1, w_in, a_ln, a_ws, a_bs, b_conv_w, b_conv_b, b_wa, b_ba,
              b_wx, b_bx, b_lam, rel_table, p_a, p_b, p_c, w_out, norm2, f_wgate, f_wup,
              f_conv_w, f_conv_b, f_wdown, final_norm):
    bsz, s, _ = x.shape
    splits = [A_WIDTH, A_WIDTH, B_WIDTH, B_WIDTH, C_WIDTH, C_WIDTH, C_WIDTH]
    offsets = [int(o) for o in np.cumsum(splits)]
    cond = jax.nn.silu(c)
    for l in range(DEPTH):
        ada = cond @ w_ada[l] + b_ada[l]
        sh1, sc1, g1, sh2, sc2, g2 = jnp.split(ada, 6, axis=-1)

        xn = _modulate(_rmsnorm(x, norm1[l]), sh1, sc1)
        z = xn @ w_in[l]
        a_u, a_v, b_x, b_g, c_q, c_k, c_v, z_gate = jnp.split(z, offsets, axis=-1)
        out_a = _mixer_a(a_u, a_v, a_ln[l], a_ws[l], a_bs[l])
        out_b = _mixer_b(b_x, b_g, b_conv_w[l], b_conv_b[l], b_wa[l], b_ba[l], b_wx[l], b_bx[l], b_lam[l])
        out_c = _mixer_c(c_q, c_k, c_v, rel_table)
        gates = jax.nn.sigmoid(z_gate.reshape(bsz, s, N_BRANCH, D_MODEL))
        merged = (gates[:, :, 0] * (out_a @ p_a[l]) + gates[:, :, 1] * (out_b @ p_b[l])
                  + gates[:, :, 2] * (out_c @ p_c[l]))
        x = x + g1[:, None, :] * (merged @ w_out[l])

        xn = _modulate(_rmsnorm(x, norm2[l]), sh2, sc2)
        x = x + g2[:, None, :] * _ffn(xn, f_wgate[l], f_wup[l], f_conv_w[l], f_conv_b[l], f_wdown[l])
    return _rmsnorm(x, final_norm)
```

```python
import functools
import math

import numpy as np
import jax
import jax.numpy as jnp
from jax import lax
from jax.experimental import pallas as pl
from jax.experimental.pallas import tpu as pltpu

F32 = jnp.float32
BF16 = jnp.bfloat16

D_MODEL = 1024
HEAD_DIM = 64
WIDTH = 768
N_HEADS = WIDTH // HEAD_DIM
A_CHUNK = 128
B_CONV = 4
RG_C = 8.0
C_CONFIGS = ((128, 1), (512, 4), (2048, 16))
ATT_BLOCK = 128
N_BUCKETS = 32
MAX_DISTANCE = 2048
N_BRANCH = 3
D_FF = 2816
FFN_CONV = 3
EPS = 1e-6
NEG_INF = -1e30
GATE_COLS = N_BRANCH * D_MODEL
IN_COLS = 7 * WIDTH + GATE_COLS

LANES = 128
SUBLANES = 8
BF16_ROWS = 16
MXU_DIM = 256

ZB_AU, ZB_AV, ZB_BX, ZB_BG, ZB_Q, ZB_K, ZB_V = (GATE_COLS // WIDTH + i for i in range(7))

VMEM_LIMIT = 56 * 1024 * 1024


def _gelu(x):
    return jax.nn.gelu(x)


def _rms_mod(x, g, shift, scale):
    y = x * lax.rsqrt(jnp.mean(x * x, axis=-1, keepdims=True) + EPS) * g
    return y * (1.0 + scale) + shift


def _ada_kernel(c_ref, w_ref, b_ref, o_ref):
    c = c_ref[...]
    cond = c * jax.nn.sigmoid(c)
    o_ref[...] = jnp.dot(cond, w_ref[...], preferred_element_type=F32,
                         precision=lax.Precision.HIGHEST) + b_ref[...]


def _ada(c, w_ada, b_ada):
    depth, d, n = w_ada.shape
    bsz = c.shape[0]
    return pl.pallas_call(
        _ada_kernel,
        out_shape=jax.ShapeDtypeStruct((depth, bsz, n), F32),
        grid=(depth, n // D_MODEL),
        in_specs=[
            pl.BlockSpec((bsz, d), lambda l, j: (0, 0)),
            pl.BlockSpec((None, d, D_MODEL), lambda l, j: (l, 0, j)),
            pl.BlockSpec((None, 1, D_MODEL), lambda l, j: (l, 0, j)),
        ],
        out_specs=pl.BlockSpec((None, bsz, D_MODEL), lambda l, j: (l, 0, j)),
        compiler_params=pltpu.CompilerParams(dimension_semantics=("arbitrary", "arbitrary")),
        name="ada",
    )(c, w_ada, b_ada.reshape(depth, 1, n))


def _t5_bucket(dist):
    max_exact = N_BUCKETS // 2
    d = np.maximum(dist, 1).astype(np.float32)
    large = max_exact + (np.log(d / max_exact) / np.log(MAX_DISTANCE / max_exact)
                         * (N_BUCKETS - max_exact)).astype(np.int32)
    large = np.minimum(large, N_BUCKETS - 1)
    return np.where(dist < max_exact, dist, large).astype(np.int32)


def _bias_tables_static():
    qi = np.arange(ATT_BLOCK)[:, None]
    kk = np.arange(2 * ATT_BLOCK)[None, :]
    dist = qi + ATT_BLOCK - kk
    buckets, valids = [], []
    for window, dil in C_CONFIGS:
        nw = window // dil
        band = (dist >= 0) & (dist <= nw)
        buckets.append(_t5_bucket(np.maximum(dist, 0) * dil))
        valids.append(np.stack([band, band & (kk >= ATT_BLOCK)]))
    return np.stack(buckets).astype(np.int32), np.stack(valids).astype(np.int32)


def _bias_kernel(rel_ref, bucket_ref, valid_ref, o_ref):
    h = pl.program_id(1)
    bk = bucket_ref[...]
    acc = jnp.zeros(bk.shape, F32)
    for b in range(N_BUCKETS):
        acc = jnp.where(bk == b, rel_ref[b, h], acc)
    o_ref[...] = jnp.where(valid_ref[...] != 0, acc, NEG_INF)


def _bias_tables(rel_table):
    buckets, valids = _bias_tables_static()
    ncfg = len(C_CONFIGS)
    blk = (ATT_BLOCK, 2 * ATT_BLOCK)
    return pl.pallas_call(
        _bias_kernel,
        out_shape=jax.ShapeDtypeStruct((ncfg, 2, N_HEADS) + blk, F32),
        grid=(ncfg, N_HEADS, 2),
        in_specs=[
            pl.BlockSpec(memory_space=pltpu.SMEM),
            pl.BlockSpec((None,) + blk, lambda c, h, v: (c, 0, 0)),
            pl.BlockSpec((None, None) + blk, lambda c, h, v: (c, v, 0, 0)),
        ],
        out_specs=pl.BlockSpec((None, None, None) + blk, lambda c, h, v: (c, v, h, 0, 0)),
        compiler_params=pltpu.CompilerParams(dimension_semantics=("arbitrary",) * 3),
        name="bias_tables",
    )(rel_table, jnp.asarray(buckets), jnp.asarray(valids))


def _inproj_kernel(x_ref, g_ref, sh_ref, sc_ref, w_ref, z_ref, xn_ref):
    @pl.when(pl.program_id(1) == 0)
    def _():
        xn_ref[...] = _rms_mod(x_ref[...], g_ref[...], sh_ref[...], sc_ref[...]).astype(BF16)

    z_ref[...] = jnp.dot(xn_ref[...], w_ref[...], preferred_element_type=F32).astype(BF16)


def _inproj(x2d, norm_g, ada3, w_in_p, seq, tm):
    n, d = x2d.shape
    ncols = w_in_p.shape[1]
    return pl.pallas_call(
        _inproj_kernel,
        out_shape=jax.ShapeDtypeStruct((n, ncols), BF16),
        grid=(n // tm, ncols // WIDTH),
        in_specs=[
            pl.BlockSpec((tm, d), lambda i, j: (i, 0)),
            pl.BlockSpec((1, d), lambda i, j: (0, 0)),
            pl.BlockSpec((None, 1, d), lambda i, j: (i * tm // seq, 0, 0)),
            pl.BlockSpec((None, 1, d), lambda i, j: (i * tm // seq, 0, 1)),
            pl.BlockSpec((d, WIDTH), lambda i, j: (0, j)),
        ],
        out_specs=pl.BlockSpec((tm, WIDTH), lambda i, j: (i, j)),
        scratch_shapes=[pltpu.VMEM((tm, d), BF16)],
        compiler_params=pltpu.CompilerParams(dimension_semantics=("arbitrary", "arbitrary"),
                                             vmem_limit_bytes=VMEM_LIMIT),
        name="inproj",
    )(x2d, norm_g, ada3, ada3, w_in_p)


def _mixa_kernel(u_ref, v_ref, ln_ref, ws_ref, bs_ref, o_ref):
    tile = u_ref.shape[0]
    row = lax.broadcasted_iota(jnp.int32, (A_CHUNK, A_CHUNK), 0)
    col = lax.broadcasted_iota(jnp.int32, (A_CHUNK, A_CHUNK), 1)
    causal = row >= col
    low_half = lax.broadcasted_iota(jnp.int32, (A_CHUNK, LANES), 1) < HEAD_DIM
    for c in range(tile // A_CHUNK):
        rows = pl.ds(c * A_CHUNK, A_CHUNK)
        u = _gelu(u_ref[rows, :].astype(F32))
        v = _gelu(v_ref[rows, :].astype(F32))
        mu = jnp.mean(v, axis=-1, keepdims=True)
        var = jnp.mean(jnp.square(v - mu), axis=-1, keepdims=True)
        vn = ((v - mu) * lax.rsqrt(var + EPS) * ln_ref[...]).astype(BF16)
        parts = []
        for p in range(WIDTH // LANES):
            vp = vn[:, p * LANES:(p + 1) * LANES]
            w0 = jnp.where(causal, ws_ref[2 * p], 0)
            w1 = jnp.where(causal, ws_ref[2 * p + 1], 0)
            r0 = jnp.dot(w0, vp, preferred_element_type=F32)
            r1 = jnp.dot(w1, vp, preferred_element_type=F32)
            parts.append(jnp.where(low_half, r0, r1))
        sv = jnp.concatenate(parts, axis=-1) + bs_ref[...]
        o_ref[rows, :] = (u * sv).astype(BF16)


def _mixa(z, a_ln, a_ws, bs_full, tile):
    n = z.shape[0]
    return pl.pallas_call(
        _mixa_kernel,
        out_shape=jax.ShapeDtypeStruct((n, WIDTH), BF16),
        grid=(n // tile,),
        in_specs=[
            pl.BlockSpec((tile, WIDTH), lambda i: (i, ZB_AU)),
            pl.BlockSpec((tile, WIDTH), lambda i: (i, ZB_AV)),
            pl.BlockSpec((1, WIDTH), lambda i: (0, 0)),
            pl.BlockSpec((N_HEADS, A_CHUNK, A_CHUNK), lambda i: (0, 0, 0)),
            pl.BlockSpec((A_CHUNK, WIDTH), lambda i: (0, 0)),
        ],
        out_specs=pl.BlockSpec((tile, WIDTH), lambda i: (i, 0)),
        compiler_params=pltpu.CompilerParams(dimension_semantics=("arbitrary",),
                                             vmem_limit_bytes=VMEM_LIMIT),
        name="mixer_a",
    )(z, z, a_ln, a_ws, bs_full)


def _mixb_kernel(x_ref, g_ref, cw_ref, cb_ref, wax_ref, ba_ref, bx_ref, lam_ref, o_ref,
                 xs_ref, a_ref, b_ref, h_ref):
    tile = x_ref.shape[0]
    first = pl.program_id(1) == 0

    @pl.when(first)
    def _():
        xs_ref[0:SUBLANES, :] = jnp.zeros((SUBLANES, WIDTH), F32)
        h_ref[...] = jnp.zeros(h_ref.shape, F32)

    x = x_ref[...].astype(F32)
    xs_ref[SUBLANES:SUBLANES + tile, :] = x
    xb = cb_ref[...] + cw_ref[B_CONV - 1:B_CONV, :] * x
    for k in range(B_CONV - 1):
        back = B_CONV - 1 - k
        xb = xb + cw_ref[k:k + 1, :] * xs_ref[pl.ds(SUBLANES - back, tile), :]
    xs_ref[0:SUBLANES, :] = xs_ref[tile:tile + SUBLANES, :]

    xbb = xb.astype(BF16)
    r_parts, i_parts = [], []
    for blk in range(WIDTH // MXU_DIM):
        ra = jnp.dot(xbb[:, blk * MXU_DIM:(blk + 1) * MXU_DIM], wax_ref[blk],
                     preferred_element_type=F32)
        r_parts.append(ra[:, :MXU_DIM])
        i_parts.append(ra[:, MXU_DIM:])
    r = jax.nn.sigmoid(jnp.concatenate(r_parts, axis=-1) + ba_ref[...])
    gate_i = jax.nn.sigmoid(jnp.concatenate(i_parts, axis=-1) + bx_ref[...])
    neg_lam = -lam_ref[...]
    softplus = jnp.maximum(neg_lam, 0.0) + jnp.log1p(jnp.exp(-jnp.abs(neg_lam)))
    a = jnp.exp((-RG_C * softplus) * r)
    mult = jnp.sqrt(1.0 - a * a)
    row = lax.broadcasted_iota(jnp.int32, (tile, WIDTH), 0)
    mult = jnp.where(jnp.logical_and(first, row == 0), 1.0, mult)
    a_ref[...] = a
    b_ref[...] = xb * gate_i * mult

    srow = lax.broadcasted_iota(jnp.int32, (SUBLANES, WIDTH), 0)

    def body(gi, h):
        rows = pl.ds(pl.multiple_of(gi * SUBLANES, SUBLANES), SUBLANES)
        av = a_ref[rows, :]
        bv = b_ref[rows, :]
        for d in (1, 2, 4):
            keep = srow >= d
            a_sh = jnp.where(keep, pltpu.roll(av, d, 0), 1.0)
            b_sh = jnp.where(keep, pltpu.roll(bv, d, 0), 0.0)
            bv = av * b_sh + bv
            av = av * a_sh
        hv = av * h + bv
        b_ref[rows, :] = hv
        return jnp.broadcast_to(hv[SUBLANES - 1:SUBLANES, :], (SUBLANES, WIDTH))

    h_ref[...] = lax.fori_loop(0, tile // SUBLANES, body, h_ref[...])
    o_ref[...] = (b_ref[...] * _gelu(g_ref[...].astype(F32))).astype(BF16)


def _mixb(z, conv_w, conv_b, wax, ba, bx, lam, bsz, seq, tile):
    n = z.shape[0]
    nt = seq // tile
    vec = pl.BlockSpec((1, WIDTH), lambda b, j: (0, 0))
    return pl.pallas_call(
        _mixb_kernel,
        out_shape=jax.ShapeDtypeStruct((n, WIDTH), BF16),
        grid=(bsz, nt),
        in_specs=[
            pl.BlockSpec((tile, WIDTH), lambda b, j: (b * nt + j, ZB_BX)),
            pl.BlockSpec((tile, WIDTH), lambda b, j: (b * nt + j, ZB_BG)),
            pl.BlockSpec((B_CONV, WIDTH), lambda b, j: (0, 0)),
            vec,
            pl.BlockSpec((WIDTH // MXU_DIM, MXU_DIM, 2 * MXU_DIM), lambda b, j: (0, 0, 0)),
            vec, vec, vec,
        ],
        out_specs=pl.BlockSpec((tile, WIDTH), lambda b, j: (b * nt + j, 0)),
        scratch_shapes=[
            pltpu.VMEM((tile + SUBLANES, WIDTH), F32),
            pltpu.VMEM((tile, WIDTH), F32),
            pltpu.VMEM((tile, WIDTH), F32),
            pltpu.VMEM((SUBLANES, WIDTH), F32),
        ],
        compiler_params=pltpu.CompilerParams(dimension_semantics=("arbitrary", "arbitrary"),
                                             vmem_limit_bytes=VMEM_LIMIT),
        name="mixer_b",
    )(z, z, conv_w, conv_b, wax, ba, bx, lam)


def _mixc_kernel(q_ref, k_ref, v_ref, bias_ref, o_ref, q32, k32, v32, m_run, l_run, acc_run):
    seq = q_ref.shape[0]
    q32[...] = q_ref[...].astype(F32) * (HEAD_DIM ** -0.5)
    k32[...] = k_ref[...].astype(F32)
    v32[...] = v_ref[...].astype(F32)
    low_half = lax.broadcasted_iota(jnp.int32, (ATT_BLOCK, LANES), 1) < HEAD_DIM

    for ci, (_, dil) in enumerate(C_CONFIGS):
        n_blk = seq // (dil * ATT_BLOCK)
        span = ATT_BLOCK * dil

        def tile_body(t, carry, ci=ci, dil=dil, span=span):
            blk = t // dil
            res = t - blk * dil
            start = blk * span + res
            prev = jnp.maximum(blk - 1, 0) * span + res
            is_first = (blk == 0).astype(jnp.int32)

            def rows(s):
                return pl.ds(s, ATT_BLOCK, stride=dil) if dil > 1 else pl.ds(s, ATT_BLOCK)

            q = q32[rows(start), :].astype(BF16)
            kk = jnp.concatenate([k32[rows(prev), :], k32[rows(start), :]], axis=0).astype(BF16)
            vv = jnp.concatenate([v32[rows(prev), :], v32[rows(start), :]], axis=0).astype(BF16)
            ms, ls, accs = [], [], []
            for hh in range(2):
                sel = low_half if hh == 0 else jnp.logical_not(low_half)
                qh = jnp.where(sel, q, jnp.zeros_like(q))
                s = lax.dot_general(qh, kk, (((1,), (1,)), ((), ())), preferred_element_type=F32)
                s = s + bias_ref[ci, is_first, hh]
                m = jnp.max(s, axis=-1, keepdims=True)
                p = jnp.exp(s - m)
                ls.append(jnp.sum(p, axis=-1, keepdims=True))
                ms.append(m)
                accs.append(jnp.dot(p.astype(BF16), vv, preferred_element_type=F32))
            m_t = jnp.where(low_half, ms[0], ms[1])
            l_t = jnp.where(low_half, ls[0], ls[1])
            acc_t = jnp.where(low_half, accs[0], accs[1])
            if ci == 0:
                m_run[rows(start), :] = m_t
                l_run[rows(start), :] = l_t
                acc_run[rows(start), :] = acc_t
            else:
                m_old = m_run[rows(start), :]
                m_new = jnp.maximum(m_old, m_t)
                w_old = jnp.exp(m_old - m_new)
                w_t = jnp.exp(m_t - m_new)
                m_run[rows(start), :] = m_new
                l_run[rows(start), :] = w_old * l_run[rows(start), :] + w_t * l_t
                acc_run[rows(start), :] = w_old * acc_run[rows(start), :] + w_t * acc_t
            return carry

        lax.fori_loop(0, n_blk * dil, tile_body, 0)

    o_ref[...] = (acc_run[...] / l_run[...]).astype(BF16)


def _mixc(z, bias, bsz, seq):
    n = z.shape[0]
    npairs = WIDTH // LANES
    per_block = WIDTH // LANES
    qkv_spec = lambda zb: pl.BlockSpec((seq, LANES), lambda p, b: (b, zb * per_block + p))
    return pl.pallas_call(
        _mixc_kernel,
        out_shape=jax.ShapeDtypeStruct((n, WIDTH), BF16),
        grid=(npairs, bsz),
        in_specs=[
            qkv_spec(ZB_Q), qkv_spec(ZB_K), qkv_spec(ZB_V),
            pl.BlockSpec((len(C_CONFIGS), 2, 2, ATT_BLOCK, 2 * ATT_BLOCK), lambda p, b: (0, 0, p, 0, 0)),
        ],
        out_specs=pl.BlockSpec((seq, LANES), lambda p, b: (b, p)),
        scratch_shapes=[pltpu.VMEM((seq, LANES), F32) for _ in range(6)],
        compiler_params=pltpu.CompilerParams(dimension_semantics=("arbitrary", "arbitrary"),
                                             vmem_limit_bytes=VMEM_LIMIT),
        name="mixer_c",
    )(z, z, z, bias)


def _merge_kernel(x_ref, oa_ref, ob_ref, oc_ref, ga_ref, gb_ref, gc_ref, g1_ref,
                  pa_ref, pb_ref, pc_ref, wo_ref, o_ref):
    merged = None
    for o_r, g_r, p_r in ((oa_ref, ga_ref, pa_ref), (ob_ref, gb_ref, pb_ref), (oc_ref, gc_ref, pc_ref)):
        term = jax.nn.sigmoid(g_r[...].astype(F32)) * jnp.dot(o_r[...], p_r[...], preferred_element_type=F32)
        merged = term if merged is None else merged + term
    y = jnp.dot(merged.astype(BF16), wo_ref[...], preferred_element_type=F32)
    o_ref[...] = x_ref[...] + g1_ref[...] * y


def _merge(x2d, out_a, out_b, out_c, z, ada3, p_a, p_b, p_c, w_out, seq, tm):
    n, d = x2d.shape
    branch = pl.BlockSpec((tm, WIDTH), lambda i: (i, 0))
    proj = pl.BlockSpec((WIDTH, d), lambda i: (0, 0))
    return pl.pallas_call(
        _merge_kernel,
        out_shape=jax.ShapeDtypeStruct((n, d), F32),
        grid=(n // tm,),
        in_specs=[
            pl.BlockSpec((tm, d), lambda i: (i, 0)),
            branch, branch, branch,
            pl.BlockSpec((tm, d), lambda i: (i, 0)),
            pl.BlockSpec((tm, d), lambda i: (i, 1)),
            pl.BlockSpec((tm, d), lambda i: (i, 2)),
            pl.BlockSpec((None, 1, d), lambda i: (i * tm // seq, 0, 2)),
            proj, proj, proj,
            pl.BlockSpec((d, d), lambda i: (0, 0)),
        ],
        out_specs=pl.BlockSpec((tm, d), lambda i: (i, 0)),
        input_output_aliases={0: 0},
        compiler_params=pltpu.CompilerParams(dimension_semantics=("arbitrary",),
                                             vmem_limit_bytes=VMEM_LIMIT),
        name="merge",
    )(x2d, out_a, out_b, out_c, z, z, z, ada3, p_a, p_b, p_c, w_out)


FFN_HALO = BF16_ROWS
FFN_FBLOCK = MXU_DIM


def _ffn_kernel(x_ref, xh_ref, g_ref, sh_ref, sc_ref, g2_ref, wg_ref, wu_ref, cw_ref, cb_ref, wd_ref,
                fin_ref, o_ref, h_ref, *, tiles_per_seq, final):
    tm = x_ref.shape[0]
    x = x_ref[...]
    xn = _rms_mod(x, g_ref[...], sh_ref[...], sc_ref[...]).astype(BF16)
    seq_start = pl.program_id(0) % tiles_per_seq == 0
    xh = _rms_mod(xh_ref[...], g_ref[...], sh_ref[...], sc_ref[...])
    xh = jnp.where(seq_start, 0.0, xh).astype(BF16)
    xe = jnp.concatenate([xh, xn], axis=0)
    for f in range(0, D_FF, FFN_FBLOCK):
        cols = slice(f, f + FFN_FBLOCK)
        ge = jnp.dot(xe, wg_ref[:, cols], preferred_element_type=F32)
        up = jnp.dot(xn, wu_ref[:, cols], preferred_element_type=F32)
        gc = cb_ref[:, cols] + cw_ref[FFN_CONV - 1:FFN_CONV, cols] * ge[FFN_HALO:, :]
        for k in range(FFN_CONV - 1):
            back = FFN_CONV - 1 - k
            gc = gc + cw_ref[k:k + 1, cols] * pltpu.roll(ge, back, 0)[FFN_HALO:, :]
        h_ref[:, cols] = (_gelu(gc) * up).astype(BF16)
    y = jnp.dot(h_ref[...], wd_ref[...], preferred_element_type=F32)
    xo = x + g2_ref[...] * y
    if final:
        xo = xo * lax.rsqrt(jnp.mean(xo * xo, axis=-1, keepdims=True) + EPS) * fin_ref[...]
    o_ref[...] = xo


def _ffn(x2d, norm_g, ada3, w_gate, w_up, conv_w, conv_b, w_down, final_g, seq, tm, final):
    n, d = x2d.shape
    tiles_per_seq = seq // tm
    halo_blocks = tm // FFN_HALO
    vec = pl.BlockSpec((1, d), lambda i: (0, 0))
    ada = lambda k: pl.BlockSpec((None, 1, d), lambda i: (i * tm // seq, 0, k))
    resident = dict(pipeline_mode=pl.Buffered(1))
    return pl.pallas_call(
        functools.partial(_ffn_kernel, tiles_per_seq=tiles_per_seq, final=final),
        out_shape=jax.ShapeDtypeStruct((n, d), F32),
        grid=(n // tm,),
        in_specs=[
            pl.BlockSpec((tm, d), lambda i: (i, 0)),
            pl.BlockSpec((FFN_HALO, d), lambda i: (jnp.maximum(i * halo_blocks - 1, 0), 0)),
            vec, ada(3), ada(4), ada(5),
            pl.BlockSpec((d, D_FF), lambda i: (0, 0), **resident),
            pl.BlockSpec((d, D_FF), lambda i: (0, 0), **resident),
            pl.BlockSpec((FFN_CONV, D_FF), lambda i: (0, 0)),
            pl.BlockSpec((1, D_FF), lambda i: (0, 0)),
            pl.BlockSpec((D_FF, d), lambda i: (0, 0), **resident),
            vec,
        ],
        out_specs=pl.BlockSpec((tm, d), lambda i: (i, 0)),
        scratch_shapes=[pltpu.VMEM((tm, D_FF), BF16)],
        compiler_params=pltpu.CompilerParams(dimension_semantics=("arbitrary",),
                                             vmem_limit_bytes=VMEM_LIMIT),
        name="ffn",
    )(x2d, x2d, norm_g, ada3, ada3, ada3, w_gate, w_up, conv_w, conv_b, w_down, final_g)


def _block_diag(w, per):
    depth, nb, r, c = w.shape
    w = w.reshape(depth, nb // per, per, r, c)
    eye = jnp.eye(per, dtype=w.dtype)
    return jnp.einsum("dbirc,ij->dbirjc", w, eye).reshape(depth, nb // per, per * r, per * c)


def kernel(x, c, w_ada, b_ada, norm1, w_in, a_ln, a_ws, a_bs, b_conv_w, b_conv_b, b_wa, b_ba, b_wx, b_bx,
           b_lam, rel_table, p_a, p_b, p_c, w_out, norm2, f_wgate, f_wup, f_conv_w, f_conv_b, f_wdown,
           final_norm):
    bsz, seq, d = x.shape
    depth = w_in.shape[0]
    n = bsz * seq

    per = MXU_DIM // HEAD_DIM
    w_in_p = jnp.concatenate([w_in[..., 7 * WIDTH:], w_in[..., :7 * WIDTH]], axis=-1).astype(BF16)
    wax = jnp.concatenate([_block_diag(b_wa, per), _block_diag(b_wx, per)], axis=-1).astype(BF16)
    a_ws_b = a_ws.astype(BF16)
    bs_full = jnp.repeat(jnp.swapaxes(a_bs, 1, 2), HEAD_DIM, axis=2)
    p_a_b, p_b_b, p_c_b, w_out_b = (w.astype(BF16) for w in (p_a, p_b, p_c, w_out))
    wg_b, wu_b, wd_b = (w.astype(BF16) for w in (f_wgate, f_wup, f_wdown))

    ada = _ada(c, w_ada, b_ada)
    bias = _bias_tables(rel_table)

    x2d = x.reshape(n, d)
    row = lambda v: v.reshape(1, -1)
    for l in range(depth):
        ada3 = ada[l].reshape(bsz, 1, 6 * d)
        z = _inproj(x2d, row(norm1[l]), ada3, w_in_p[l], seq, tm=1024)
        out_a = _mixa(z, row(a_ln[l]), a_ws_b[l], bs_full[l], tile=512)
        out_b = _mixb(z, b_conv_w[l], row(b_conv_b[l]), wax[l], row(b_ba[l]), row(b_bx[l]), row(b_lam[l]),
                      bsz, seq, tile=512)
        out_c = _mixc(z, bias, bsz, seq)
        x2d = _merge(x2d, out_a, out_b, out_c, z, ada3, p_a_b[l], p_b_b[l], p_c_b[l], w_out_b[l], seq, tm=512)
        x2d = _ffn(x2d, row(norm2[l]), ada3, wg_b[l], wu_b[l], f_conv_w[l], row(f_conv_b[l]), wd_b[l],
                   row(final_norm), seq, tm=512, final=(l == depth - 1))
    return x2d.reshape(bsz, seq, d)
```

```python
import functools
import math

import numpy as np
import jax
import jax.numpy as jnp
from jax import lax
from jax.experimental import pallas as pl
from jax.experimental.pallas import tpu as pltpu

F32 = jnp.float32
BF16 = jnp.bfloat16

D_MODEL = 1024
HEAD_DIM = 64
WIDTH = 768
N_HEADS = WIDTH // HEAD_DIM
A_CHUNK = 128
B_CONV = 4
RG_C = 8.0
C_CONFIGS = ((128, 1), (512, 4), (2048, 16))
ATT_BLOCK = 128
N_BUCKETS = 32
MAX_DISTANCE = 2048
N_BRANCH = 3
D_FF = 2816
FFN_CONV = 3
EPS = 1e-6
NEG_INF = -1e30
GATE_COLS = N_BRANCH * D_MODEL
IN_COLS = 7 * WIDTH + GATE_COLS

LANES = 128
SUBLANES = 8
BF16_ROWS = 16
MXU_DIM = 256

ZB_AU, ZB_AV, ZB_BX, ZB_BG, ZB_Q, ZB_K, ZB_V = (GATE_COLS // WIDTH + i for i in range(7))

VMEM_LIMIT = 56 * 1024 * 1024


def _gelu(x):
    return jax.nn.gelu(x)


def _rms_mod(x, g, shift, scale):
    y = x * lax.rsqrt(jnp.mean(x * x, axis=-1, keepdims=True) + EPS) * g
    return y * (1.0 + scale) + shift


def _ada_kernel(c_ref, w_ref, b_ref, o_ref):
    c = c_ref[...]
    cond = c * jax.nn.sigmoid(c)
    o_ref[...] = jnp.dot(cond, w_ref[...], preferred_element_type=F32,
                         precision=lax.Precision.HIGHEST) + b_ref[...]


def _ada(c, w_ada, b_ada):
    depth, d, n = w_ada.shape
    bsz = c.shape[0]
    return pl.pallas_call(
        _ada_kernel,
        out_shape=jax.ShapeDtypeStruct((depth, bsz, n), F32),
        grid=(depth, n // D_MODEL),
        in_specs=[
            pl.BlockSpec((bsz, d), lambda l, j: (0, 0)),
            pl.BlockSpec((None, d, D_MODEL), lambda l, j: (l, 0, j)),
            pl.BlockSpec((None, 1, D_MODEL), lambda l, j: (l, 0, j)),
        ],
        out_specs=pl.BlockSpec((None, bsz, D_MODEL), lambda l, j: (l, 0, j)),
        compiler_params=pltpu.CompilerParams(dimension_semantics=("arbitrary", "arbitrary")),
        name="ada",
    )(c, w_ada, b_ada.reshape(depth, 1, n))


LOG2E = math.log2(math.e)


def _t5_bucket(dist):
    max_exact = N_BUCKETS // 2
    d = np.maximum(dist, 1).astype(np.float32)
    large = max_exact + (np.log(d / max_exact) / np.log(MAX_DISTANCE / max_exact)
                         * (N_BUCKETS - max_exact)).astype(np.int32)
    large = np.minimum(large, N_BUCKETS - 1)
    return np.where(dist < max_exact, dist, large).astype(np.int32)


def _bias_tables_static():
    qi = np.arange(ATT_BLOCK)[:, None]
    kk = np.arange(2 * ATT_BLOCK)[None, :]
    dist = qi + ATT_BLOCK - kk
    tables = []
    for window, dil in C_CONFIGS:
        band = (dist >= 0) & (dist <= window // dil)
        tables.append(np.where(band, _t5_bucket(np.maximum(dist, 0) * dil), -1))
    return np.stack(tables).astype(np.int32)


def _bias_kernel(rel_ref, bucket_ref, o_ref):
    h = pl.program_id(1)
    bk = bucket_ref[...]
    acc = jnp.full(bk.shape, NEG_INF, F32)
    for b in range(N_BUCKETS):
        acc = jnp.where(bk == b, rel_ref[b, h] * LOG2E, acc)
    o_ref[...] = acc


def _bias_tables(rel_table):
    buckets = _bias_tables_static()
    ncfg = len(C_CONFIGS)
    blk = (ATT_BLOCK, 2 * ATT_BLOCK)
    return pl.pallas_call(
        _bias_kernel,
        out_shape=jax.ShapeDtypeStruct((ncfg, N_HEADS) + blk, F32),
        grid=(ncfg, N_HEADS),
        in_specs=[
            pl.BlockSpec(memory_space=pltpu.SMEM),
            pl.BlockSpec((None,) + blk, lambda c, h: (c, 0, 0)),
        ],
        out_specs=pl.BlockSpec((None, None) + blk, lambda c, h: (c, h, 0, 0)),
        compiler_params=pltpu.CompilerParams(dimension_semantics=("arbitrary",) * 2),
        name="bias_tables",
    )(rel_table, jnp.asarray(buckets))


INPROJ_COLS = IN_COLS // 3


def _inproj_kernel(x_ref, g_ref, sh_ref, sc_ref, w_ref, z_ref):
    xn = _rms_mod(x_ref[...], g_ref[...], sh_ref[...], sc_ref[...]).astype(BF16)
    z_ref[...] = jnp.dot(xn, w_ref[...], preferred_element_type=F32).astype(BF16)


def _inproj(x2d, norm_g, ada3, w_in_p, seq, tm):
    n, d = x2d.shape
    ncols = w_in_p.shape[1]
    return pl.pallas_call(
        _inproj_kernel,
        out_shape=jax.ShapeDtypeStruct((n, ncols), BF16),
        grid=(n // tm, ncols // INPROJ_COLS),
        in_specs=[
            pl.BlockSpec((tm, d), lambda i, j: (i, 0)),
            pl.BlockSpec((1, d), lambda i, j: (0, 0)),
            pl.BlockSpec((None, 1, d), lambda i, j: (i * tm // seq, 0, 0)),
            pl.BlockSpec((None, 1, d), lambda i, j: (i * tm // seq, 0, 1)),
            pl.BlockSpec((d, INPROJ_COLS), lambda i, j: (0, j)),
        ],
        out_specs=pl.BlockSpec((tm, INPROJ_COLS), lambda i, j: (i, j)),
        compiler_params=pltpu.CompilerParams(dimension_semantics=("arbitrary", "arbitrary"),
                                             vmem_limit_bytes=VMEM_LIMIT),
        name="inproj",
    )(x2d, norm_g, ada3, ada3, w_in_p)


def _mixa_kernel(u_ref, v_ref, ln_ref, ws_ref, bs_ref, o_ref):
    tile = u_ref.shape[0]
    row = lax.broadcasted_iota(jnp.int32, (A_CHUNK, A_CHUNK), 0)
    col = lax.broadcasted_iota(jnp.int32, (A_CHUNK, A_CHUNK), 1)
    causal = row >= col
    low_half = lax.broadcasted_iota(jnp.int32, (A_CHUNK, LANES), 1) < HEAD_DIM
    for c in range(tile // A_CHUNK):
        rows = pl.ds(c * A_CHUNK, A_CHUNK)
        u = _gelu(u_ref[rows, :].astype(F32))
        v = _gelu(v_ref[rows, :].astype(F32))
        mu = jnp.mean(v, axis=-1, keepdims=True)
        var = jnp.mean(jnp.square(v - mu), axis=-1, keepdims=True)
        vn = ((v - mu) * lax.rsqrt(var + EPS) * ln_ref[...]).astype(BF16)
        parts = []
        for p in range(WIDTH // LANES):
            vp = vn[:, p * LANES:(p + 1) * LANES]
            w0 = jnp.where(causal, ws_ref[2 * p], 0)
            w1 = jnp.where(causal, ws_ref[2 * p + 1], 0)
            r0 = jnp.dot(w0, vp, preferred_element_type=F32)
            r1 = jnp.dot(w1, vp, preferred_element_type=F32)
            parts.append(jnp.where(low_half, r0, r1))
        sv = jnp.concatenate(parts, axis=-1) + bs_ref[...]
        o_ref[rows, :] = (u * sv).astype(BF16)


def _mixa(z, a_ln, a_ws, bs_full, tile):
    n = z.shape[0]
    return pl.pallas_call(
        _mixa_kernel,
        out_shape=jax.ShapeDtypeStruct((n, WIDTH), BF16),
        grid=(n // tile,),
        in_specs=[
            pl.BlockSpec((tile, WIDTH), lambda i: (i, ZB_AU)),
            pl.BlockSpec((tile, WIDTH), lambda i: (i, ZB_AV)),
            pl.BlockSpec((1, WIDTH), lambda i: (0, 0)),
            pl.BlockSpec((N_HEADS, A_CHUNK, A_CHUNK), lambda i: (0, 0, 0)),
            pl.BlockSpec((A_CHUNK, WIDTH), lambda i: (0, 0)),
        ],
        out_specs=pl.BlockSpec((tile, WIDTH), lambda i: (i, 0)),
        compiler_params=pltpu.CompilerParams(dimension_semantics=("arbitrary",),
                                             vmem_limit_bytes=VMEM_LIMIT),
        name="mixer_a",
    )(z, z, a_ln, a_ws, bs_full)


def _mixb_kernel(x_ref, g_ref, cw_ref, cb_ref, wax_ref, ba_ref, bx_ref, lam_ref, o_ref,
                 xs_ref, a_ref, b_ref, h_ref):
    tile = x_ref.shape[0]
    first = pl.program_id(1) == 0

    @pl.when(first)
    def _():
        xs_ref[0:SUBLANES, :] = jnp.zeros((SUBLANES, WIDTH), F32)
        h_ref[...] = jnp.zeros(h_ref.shape, F32)

    x = x_ref[...].astype(F32)
    xs_ref[SUBLANES:SUBLANES + tile, :] = x
    xb = cb_ref[...] + cw_ref[B_CONV - 1:B_CONV, :] * x
    for k in range(B_CONV - 1):
        back = B_CONV - 1 - k
        xb = xb + cw_ref[k:k + 1, :] * xs_ref[pl.ds(SUBLANES - back, tile), :]
    xs_ref[0:SUBLANES, :] = xs_ref[tile:tile + SUBLANES, :]

    xbb = xb.astype(BF16)
    r_parts, i_parts = [], []
    for blk in range(WIDTH // MXU_DIM):
        ra = jnp.dot(xbb[:, blk * MXU_DIM:(blk + 1) * MXU_DIM], wax_ref[blk],
                     preferred_element_type=F32)
        r_parts.append(ra[:, :MXU_DIM])
        i_parts.append(ra[:, MXU_DIM:])
    r = jax.nn.sigmoid(jnp.concatenate(r_parts, axis=-1) + ba_ref[...])
    gate_i = jax.nn.sigmoid(jnp.concatenate(i_parts, axis=-1) + bx_ref[...])
    neg_lam = -lam_ref[...]
    softplus = jnp.maximum(neg_lam, 0.0) + jnp.log1p(jnp.exp(-jnp.abs(neg_lam)))
    a = jnp.exp((-RG_C * softplus) * r)
    mult = jnp.sqrt(1.0 - a * a)
    row = lax.broadcasted_iota(jnp.int32, (tile, WIDTH), 0)
    mult = jnp.where(jnp.logical_and(first, row == 0), 1.0, mult)
    a_ref[...] = a
    b_ref[...] = xb * gate_i * mult

    srow = lax.broadcasted_iota(jnp.int32, (SUBLANES, WIDTH), 0)

    def body(gi, h):
        rows = pl.ds(pl.multiple_of(gi * SUBLANES, SUBLANES), SUBLANES)
        av = a_ref[rows, :]
        bv = b_ref[rows, :]
        for d in (1, 2, 4):
            keep = srow >= d
            a_sh = jnp.where(keep, pltpu.roll(av, d, 0), 1.0)
            b_sh = jnp.where(keep, pltpu.roll(bv, d, 0), 0.0)
            bv = av * b_sh + bv
            av = av * a_sh
        hv = av * h + bv
        b_ref[rows, :] = hv
        return jnp.broadcast_to(hv[SUBLANES - 1:SUBLANES, :], (SUBLANES, WIDTH))

    h_ref[...] = lax.fori_loop(0, tile // SUBLANES, body, h_ref[...])
    o_ref[...] = (b_ref[...] * _gelu(g_ref[...].astype(F32))).astype(BF16)


def _mixb(z, conv_w, conv_b, wax, ba, bx, lam, bsz, seq, tile):
    n = z.shape[0]
    nt = seq // tile
    vec = pl.BlockSpec((1, WIDTH), lambda b, j: (0, 0))
    return pl.pallas_call(
        _mixb_kernel,
        out_shape=jax.ShapeDtypeStruct((n, WIDTH), BF16),
        grid=(bsz, nt),
        in_specs=[
            pl.BlockSpec((tile, WIDTH), lambda b, j: (b * nt + j, ZB_BX)),
            pl.BlockSpec((tile, WIDTH), lambda b, j: (b * nt + j, ZB_BG)),
            pl.BlockSpec((B_CONV, WIDTH), lambda b, j: (0, 0)),
            vec,
            pl.BlockSpec((WIDTH // MXU_DIM, MXU_DIM, 2 * MXU_DIM), lambda b, j: (0, 0, 0)),
            vec, vec, vec,
        ],
        out_specs=pl.BlockSpec((tile, WIDTH), lambda b, j: (b * nt + j, 0)),
        scratch_shapes=[
            pltpu.VMEM((tile + SUBLANES, WIDTH), F32),
            pltpu.VMEM((tile, WIDTH), F32),
            pltpu.VMEM((tile, WIDTH), F32),
            pltpu.VMEM((SUBLANES, WIDTH), F32),
        ],
        compiler_params=pltpu.CompilerParams(dimension_semantics=("arbitrary", "arbitrary"),
                                             vmem_limit_bytes=VMEM_LIMIT),
        name="mixer_b",
    )(z, z, conv_w, conv_b, wax, ba, bx, lam)


ATT_GROUP_MAX = 5


def _group_size(n):
    return max(g for g in range(1, ATT_GROUP_MAX + 1) if n % g == 0)


def _mixc_kernel(q_ref, k_ref, v_ref, bias_ref, o_ref, qb, q32, k32, v32, m_st, l_st, acc_st):
    seq = q_ref.shape[0]
    ncfg = len(C_CONFIGS)
    qs = q_ref[...].astype(F32) * (LOG2E * HEAD_DIM ** -0.5)
    q32[...] = qs
    qb[...] = qs.astype(BF16)
    k32[...] = k_ref[...].astype(F32)
    v32[...] = v_ref[...].astype(F32)
    low_half = lax.broadcasted_iota(jnp.int32, (ATT_BLOCK, LANES), 1) < HEAD_DIM

    def tiles(ci, dil, starts, with_prev):
        span = ATT_BLOCK * dil

        def rows(s):
            if dil > 1:
                return pl.ds(s, ATT_BLOCK, stride=dil)
            return pl.ds(pl.multiple_of(s, ATT_BLOCK), ATT_BLOCK)

        def load(ref_b, ref_32, s):
            return ref_b[rows(s), :] if dil == 1 else ref_32[rows(s), :].astype(BF16)

        bias = bias_ref[ci] if with_prev else bias_ref[ci, :, ATT_BLOCK:]
        operands = []
        for start in starts:
            q = load(qb, q32, start)
            zero = jnp.zeros_like(q)
            lhs = jnp.concatenate([jnp.where(low_half, q, zero), jnp.where(low_half, zero, q)], axis=0)
            kk = load(k_ref, k32, start)
            vv = load(v_ref, v32, start)
            if with_prev:
                kk = jnp.concatenate([load(k_ref, k32, start - span), kk], axis=0)
                vv = jnp.concatenate([load(v_ref, v32, start - span), vv], axis=0)
            operands.append((lhs, kk, jnp.concatenate([vv, jnp.ones_like(vv)], axis=1)))
        logits = [lax.dot_general(lhs, kk, (((1,), (1,)), ((), ())), preferred_element_type=F32)
                  for lhs, kk, _ in operands]
        probs = []
        for s in logits:
            s = s + bias
            m = jnp.max(s, axis=-1, keepdims=True)
            probs.append((m, jnp.exp2(s - m).astype(BF16)))
        results = [jnp.dot(p, v_aug, preferred_element_type=F32)
                   for (_, p), (_, _, v_aug) in zip(probs, operands)]
        for start, (m, _), r in zip(starts, probs, results):
            m_st[ci, rows(start), :] = jnp.where(low_half, m[:ATT_BLOCK], m[ATT_BLOCK:])
            l_st[ci, rows(start), :] = jnp.where(low_half, r[:ATT_BLOCK, LANES:], r[ATT_BLOCK:, LANES:])
            acc_st[ci, rows(start), :] = jnp.where(low_half, r[:ATT_BLOCK, :LANES], r[ATT_BLOCK:, :LANES])

    for ci, (_, dil) in enumerate(C_CONFIGS):
        n_blk = seq // (dil * ATT_BLOCK)
        first = _group_size(dil)
        for r0 in range(0, dil, first):
            tiles(ci, dil, list(range(r0, r0 + first)), with_prev=False)
        n_inner = (n_blk - 1) * dil
        if n_inner == 0:
            continue
        group = _group_size(n_inner)

        def body(g, carry, ci=ci, dil=dil, group=group):
            starts = []
            for u in range(group):
                t = dil + g * group + u
                blk = t // dil
                starts.append(blk * (ATT_BLOCK * dil) + (t - blk * dil))
            tiles(ci, dil, starts, with_prev=True)
            return carry

        lax.fori_loop(0, n_inner // group, body, 0)

    ms = [m_st[ci] for ci in range(ncfg)]
    top = functools.reduce(jnp.maximum, ms)
    num = jnp.zeros((seq, LANES), F32)
    den = jnp.zeros((seq, LANES), F32)
    for ci in range(ncfg):
        w = jnp.exp2(ms[ci] - top)
        num = num + w * acc_st[ci]
        den = den + w * l_st[ci]
    o_ref[...] = (num / den).astype(BF16)


def _mixc(z, bias, bsz, seq):
    n = z.shape[0]
    npairs = WIDTH // LANES
    ncfg = len(C_CONFIGS)
    bias = bias.reshape(ncfg, npairs, 2 * ATT_BLOCK, 2 * ATT_BLOCK)
    qkv_spec = lambda zb: pl.BlockSpec((seq, LANES), lambda p, b: (b, zb * npairs + p))
    return pl.pallas_call(
        _mixc_kernel,
        out_shape=jax.ShapeDtypeStruct((n, WIDTH), BF16),
        grid=(npairs, bsz),
        in_specs=[
            qkv_spec(ZB_Q), qkv_spec(ZB_K), qkv_spec(ZB_V),
            pl.BlockSpec((ncfg, None, 2 * ATT_BLOCK, 2 * ATT_BLOCK), lambda p, b: (0, p, 0, 0)),
        ],
        out_specs=pl.BlockSpec((seq, LANES), lambda p, b: (b, p)),
        scratch_shapes=[pltpu.VMEM((seq, LANES), BF16)]
        + [pltpu.VMEM((seq, LANES), F32) for _ in range(3)]
        + [pltpu.VMEM((ncfg, seq, LANES), F32) for _ in range(3)],
        compiler_params=pltpu.CompilerParams(dimension_semantics=("arbitrary", "arbitrary"),
                                             vmem_limit_bytes=VMEM_LIMIT),
        name="mixer_c",
    )(z, z, z, bias)


def _merge_kernel(x_ref, oa_ref, ob_ref, oc_ref, ga_ref, gb_ref, gc_ref, g1_ref,
                  pa_ref, pb_ref, pc_ref, wo_ref, o_ref):
    merged = None
    for o_r, g_r, p_r in ((oa_ref, ga_ref, pa_ref), (ob_ref, gb_ref, pb_ref), (oc_ref, gc_ref, pc_ref)):
        term = jax.nn.sigmoid(g_r[...].astype(F32)) * jnp.dot(o_r[...], p_r[...], preferred_element_type=F32)
        merged = term if merged is None else merged + term
    y = jnp.dot(merged.astype(BF16), wo_ref[...], preferred_element_type=F32)
    o_ref[...] = x_ref[...] + g1_ref[...] * y


def _merge(x2d, out_a, out_b, out_c, z, ada3, p_a, p_b, p_c, w_out, seq, tm):
    n, d = x2d.shape
    branch = pl.BlockSpec((tm, WIDTH), lambda i: (i, 0))
    proj = pl.BlockSpec((WIDTH, d), lambda i: (0, 0))
    return pl.pallas_call(
        _merge_kernel,
        out_shape=jax.ShapeDtypeStruct((n, d), F32),
        grid=(n // tm,),
        in_specs=[
            pl.BlockSpec((tm, d), lambda i: (i, 0)),
            branch, branch, branch,
            pl.BlockSpec((tm, d), lambda i: (i, 0)),
            pl.BlockSpec((tm, d), lambda i: (i, 1)),
            pl.BlockSpec((tm, d), lambda i: (i, 2)),
            pl.BlockSpec((None, 1, d), lambda i: (i * tm // seq, 0, 2)),
            proj, proj, proj,
            pl.BlockSpec((d, d), lambda i: (0, 0)),
        ],
        out_specs=pl.BlockSpec((tm, d), lambda i: (i, 0)),
        input_output_aliases={0: 0},
        compiler_params=pltpu.CompilerParams(dimension_semantics=("arbitrary",),
                                             vmem_limit_bytes=VMEM_LIMIT),
        name="merge",
    )(x2d, out_a, out_b, out_c, z, z, z, ada3, p_a, p_b, p_c, w_out)


FFN_HALO = BF16_ROWS
FFN_FBLOCK = MXU_DIM


def _ffn_kernel(x_ref, xh_ref, g_ref, sh_ref, sc_ref, g2_ref, wg_ref, wu_ref, cw_ref, cb_ref, wd_ref,
                fin_ref, o_ref, h_ref, *, tiles_per_seq, final):
    tm = x_ref.shape[0]
    x = x_ref[...]
    xn = _rms_mod(x, g_ref[...], sh_ref[...], sc_ref[...]).astype(BF16)
    seq_start = pl.program_id(0) % tiles_per_seq == 0
    xh = _rms_mod(xh_ref[...], g_ref[...], sh_ref[...], sc_ref[...])
    xh = jnp.where(seq_start, 0.0, xh).astype(BF16)
    xe = jnp.concatenate([xh, xn], axis=0)
    for f in range(0, D_FF, FFN_FBLOCK):
        cols = slice(f, f + FFN_FBLOCK)
        ge = jnp.dot(xe, wg_ref[:, cols], preferred_element_type=F32)
        up = jnp.dot(xn, wu_ref[:, cols], preferred_element_type=F32)
        gc = cb_ref[:, cols] + cw_ref[FFN_CONV - 1:FFN_CONV, cols] * ge[FFN_HALO:, :]
        for k in range(FFN_CONV - 1):
            back = FFN_CONV - 1 - k
            gc = gc + cw_ref[k:k + 1, cols] * pltpu.roll(ge, back, 0)[FFN_HALO:, :]
        h_ref[:, cols] = (_gelu(gc) * up).astype(BF16)
    y = jnp.dot(h_ref[...], wd_ref[...], preferred_element_type=F32)
    xo = x + g2_ref[...] * y
    if final:
        xo = xo * lax.rsqrt(jnp.mean(xo * xo, axis=-1, keepdims=True) + EPS) * fin_ref[...]
    o_ref[...] = xo


def _ffn(x2d, norm_g, ada3, w_gate, w_up, conv_w, conv_b, w_down, final_g, seq, tm, final):
    n, d = x2d.shape
    tiles_per_seq = seq // tm
    halo_blocks = tm // FFN_HALO
    vec = pl.BlockSpec((1, d), lambda i: (0, 0))
    ada = lambda k: pl.BlockSpec((None, 1, d), lambda i: (i * tm // seq, 0, k))
    resident = dict(pipeline_mode=pl.Buffered(1))
    return pl.pallas_call(
        functools.partial(_ffn_kernel, tiles_per_seq=tiles_per_seq, final=final),
        out_shape=jax.ShapeDtypeStruct((n, d), F32),
        grid=(n // tm,),
        in_specs=[
            pl.BlockSpec((tm, d), lambda i: (i, 0)),
            pl.BlockSpec((FFN_HALO, d), lambda i: (jnp.maximum(i * halo_blocks - 1, 0), 0)),
            vec, ada(3), ada(4), ada(5),
            pl.BlockSpec((d, D_FF), lambda i: (0, 0), **resident),
            pl.BlockSpec((d, D_FF), lambda i: (0, 0), **resident),
            pl.BlockSpec((FFN_CONV, D_FF), lambda i: (0, 0)),
            pl.BlockSpec((1, D_FF), lambda i: (0, 0)),
            pl.BlockSpec((D_FF, d), lambda i: (0, 0), **resident),
            vec,
        ],
        out_specs=pl.BlockSpec((tm, d), lambda i: (i, 0)),
        scratch_shapes=[pltpu.VMEM((tm, D_FF), BF16)],
        compiler_params=pltpu.CompilerParams(dimension_semantics=("arbitrary",),
                                             vmem_limit_bytes=VMEM_LIMIT),
        name="ffn",
    )(x2d, x2d, norm_g, ada3, ada3, ada3, w_gate, w_up, conv_w, conv_b, w_down, final_g)


def _block_diag(w, per):
    depth, nb, r, c = w.shape
    w = w.reshape(depth, nb // per, per, r, c)
    eye = jnp.eye(per, dtype=w.dtype)
    return jnp.einsum("dbirc,ij->dbirjc", w, eye).reshape(depth, nb // per, per * r, per * c)


def kernel(x, c, w_ada, b_ada, norm1, w_in, a_ln, a_ws, a_bs, b_conv_w, b_conv_b, b_wa, b_ba, b_wx, b_bx,
           b_lam, rel_table, p_a, p_b, p_c, w_out, norm2, f_wgate, f_wup, f_conv_w, f_conv_b, f_wdown,
           final_norm):
    bsz, seq, d = x.shape
    depth = w_in.shape[0]
    n = bsz * seq

    per = MXU_DIM // HEAD_DIM
    w_in_p = jnp.concatenate([w_in[..., 7 * WIDTH:], w_in[..., :7 * WIDTH]], axis=-1).astype(BF16)
    wax = jnp.concatenate([_block_diag(b_wa, per), _block_diag(b_wx, per)], axis=-1).astype(BF16)
    a_ws_b = a_ws.astype(BF16)
    bs_full = jnp.repeat(jnp.swapaxes(a_bs, 1, 2), HEAD_DIM, axis=2)
    p_a_b, p_b_b, p_c_b, w_out_b = (w.astype(BF16) for w in (p_a, p_b, p_c, w_out))
    wg_b, wu_b, wd_b = (w.astype(BF16) for w in (f_wgate, f_wup, f_wdown))

    ada = _ada(c, w_ada, b_ada)
    bias = _bias_tables(rel_table)

    x2d = x.reshape(n, d)
    row = lambda v: v.reshape(1, -1)
    for l in range(depth):
        ada3 = ada[l].reshape(bsz, 1, 6 * d)
        z = _inproj(x2d, row(norm1[l]), ada3, w_in_p[l], seq, tm=1024)
        out_a = _mixa(z, row(a_ln[l]), a_ws_b[l], bs_full[l], tile=512)
        out_b = _mixb(z, b_conv_w[l], row(b_conv_b[l]), wax[l], row(b_ba[l]), row(b_bx[l]), row(b_lam[l]),
                      bsz, seq, tile=512)
        out_c = _mixc(z, bias, bsz, seq)
        x2d = _merge(x2d, out_a, out_b, out_c, z, ada3, p_a_b[l], p_b_b[l], p_c_b[l], w_out_b[l], seq, tm=512)
        x2d = _ffn(x2d, row(norm2[l]), ada3, wg_b[l], wu_b[l], f_conv_w[l], row(f_conv_b[l]), wd_b[l],
                   row(final_norm), seq, tm=512, final=(l == depth - 1))
    return x2d.reshape(bsz, seq, d)
```

```python
import functools
import math

import numpy as np
import jax
import jax.numpy as jnp
from jax import lax
from jax.experimental import pallas as pl
from jax.experimental.pallas import tpu as pltpu

F32 = jnp.float32
BF16 = jnp.bfloat16

D_MODEL = 1024
HEAD_DIM = 64
WIDTH = 768
N_HEADS = WIDTH // HEAD_DIM
A_CHUNK = 128
B_CONV = 4
RG_C = 8.0
C_CONFIGS = ((128, 1), (512, 4), (2048, 16))
ATT_BLOCK = 128
N_BUCKETS = 32
MAX_DISTANCE = 2048
N_BRANCH = 3
D_FF = 2816
FFN_CONV = 3
EPS = 1e-6
NEG_INF = -1e30
GATE_COLS = N_BRANCH * D_MODEL
IN_COLS = 7 * WIDTH + GATE_COLS
LOG2E = math.log2(math.e)
Q_SCALE = LOG2E * HEAD_DIM ** -0.5

LANES = 128
SUBLANES = 8
BF16_ROWS = 16
MXU_DIM = 256

_Z_ORDER = ("gates", "a_u", "a_v", "b_g", "b_x", "q", "k", "v")
_W_IN_OFFSET = {"a_u": 0, "a_v": WIDTH, "b_x": 2 * WIDTH, "b_g": 3 * WIDTH, "q": 4 * WIDTH, "k": 5 * WIDTH,
                "v": 6 * WIDTH, "gates": 7 * WIDTH}
_Z_ACT = {"gates": "sigmoid", "a_u": "gelu", "a_v": "gelu", "b_g": "gelu", "b_x": "id", "q": "qscale",
          "k": "id", "v": "id"}
_Z_WIDTH = {name: (GATE_COLS if name == "gates" else WIDTH) for name in _Z_ORDER}
_Z_START = {name: sum(_Z_WIDTH[m] for m in _Z_ORDER[:i]) for i, name in enumerate(_Z_ORDER)}
ZB_AU, ZB_AV, ZB_BG, ZB_BX, ZB_Q, ZB_K, ZB_V = (_Z_START[m] // WIDTH for m in _Z_ORDER[1:])

VMEM_LIMIT = 56 * 1024 * 1024


def _gelu(x):
    return jax.nn.gelu(x)


def _rms_mod(x, g, shift, scale):
    y = x * lax.rsqrt(jnp.mean(x * x, axis=-1, keepdims=True) + EPS) * g
    return y * (1.0 + scale) + shift


def _ada_kernel(c_ref, w_ref, b_ref, o_ref):
    c = c_ref[...]
    cond = c * jax.nn.sigmoid(c)
    o_ref[...] = jnp.dot(cond, w_ref[...], preferred_element_type=F32,
                         precision=lax.Precision.HIGHEST) + b_ref[...]


def _ada(c, w_ada, b_ada):
    depth, d, n = w_ada.shape
    bsz = c.shape[0]
    return pl.pallas_call(
        _ada_kernel,
        out_shape=jax.ShapeDtypeStruct((depth, bsz, n), F32),
        grid=(depth, n // D_MODEL),
        in_specs=[
            pl.BlockSpec((bsz, d), lambda l, j: (0, 0)),
            pl.BlockSpec((None, d, D_MODEL), lambda l, j: (l, 0, j)),
            pl.BlockSpec((None, 1, D_MODEL), lambda l, j: (l, 0, j)),
        ],
        out_specs=pl.BlockSpec((None, bsz, D_MODEL), lambda l, j: (l, 0, j)),
        compiler_params=pltpu.CompilerParams(dimension_semantics=("arbitrary", "arbitrary")),
        name="ada",
    )(c, w_ada, b_ada.reshape(depth, 1, n))


def _t5_bucket(dist):
    max_exact = N_BUCKETS // 2
    d = np.maximum(dist, 1).astype(np.float32)
    large = max_exact + (np.log(d / max_exact) / np.log(MAX_DISTANCE / max_exact)
                         * (N_BUCKETS - max_exact)).astype(np.int32)
    large = np.minimum(large, N_BUCKETS - 1)
    return np.where(dist < max_exact, dist, large).astype(np.int32)


def _bias_tables_static():
    qi = np.arange(ATT_BLOCK)[:, None]
    kk = np.arange(2 * ATT_BLOCK)[None, :]
    dist = qi + ATT_BLOCK - kk
    tables = []
    for window, dil in C_CONFIGS:
        band = (dist >= 0) & (dist <= window // dil)
        tables.append(np.where(band, _t5_bucket(np.maximum(dist, 0) * dil), -1))
    return np.stack(tables).astype(np.int32)


def _bias_kernel(rel_ref, bucket_ref, o_ref):
    h = pl.program_id(1)
    bk = bucket_ref[...]
    acc = jnp.full(bk.shape, NEG_INF, F32)
    for b in range(N_BUCKETS):
        acc = jnp.where(bk == b, rel_ref[b, h] * LOG2E, acc)
    o_ref[...] = acc


def _bias_tables(rel_table):
    buckets = _bias_tables_static()
    ncfg = len(C_CONFIGS)
    blk = (ATT_BLOCK, 2 * ATT_BLOCK)
    return pl.pallas_call(
        _bias_kernel,
        out_shape=jax.ShapeDtypeStruct((ncfg, N_HEADS) + blk, F32),
        grid=(ncfg, N_HEADS),
        in_specs=[
            pl.BlockSpec(memory_space=pltpu.SMEM),
            pl.BlockSpec((None,) + blk, lambda c, h: (c, 0, 0)),
        ],
        out_specs=pl.BlockSpec((None, None) + blk, lambda c, h: (c, h, 0, 0)),
        compiler_params=pltpu.CompilerParams(dimension_semantics=("arbitrary",) * 2),
        name="bias_tables",
    )(rel_table, jnp.asarray(buckets))


INPROJ_BLOCKS = 3
INPROJ_COLS = IN_COLS // INPROJ_BLOCKS


def _inproj_segments(block):
    lo, hi = block * INPROJ_COLS, (block + 1) * INPROJ_COLS
    segs = []
    for name in _Z_ORDER:
        a, b = max(_Z_START[name], lo), min(_Z_START[name] + _Z_WIDTH[name], hi)
        if a < b:
            if segs and segs[-1][2] == _Z_ACT[name]:
                segs[-1] = (segs[-1][0], b - lo, _Z_ACT[name])
            else:
                segs.append((a - lo, b - lo, _Z_ACT[name]))
    return segs


def _inproj_kernel(x_ref, g_ref, sh_ref, sc_ref, w_ref, z_ref):
    acts = {"sigmoid": jax.nn.sigmoid, "gelu": _gelu, "id": lambda t: t, "qscale": lambda t: t * Q_SCALE}
    for block in range(INPROJ_BLOCKS):
        @pl.when(pl.program_id(1) == block)
        def _(block=block):
            xn = _rms_mod(x_ref[...], g_ref[...], sh_ref[...], sc_ref[...]).astype(BF16)
            for a, b, act in _inproj_segments(block):
                acc = jnp.dot(xn, w_ref[:, a:b], preferred_element_type=F32)
                z_ref[:, a:b] = acts[act](acc).astype(BF16)


def _inproj(x2d, norm_g, ada3, w_in_p, seq, tm):
    n, d = x2d.shape
    return pl.pallas_call(
        _inproj_kernel,
        out_shape=jax.ShapeDtypeStruct((n, IN_COLS), BF16),
        grid=(n // tm, INPROJ_BLOCKS),
        in_specs=[
            pl.BlockSpec((tm, d), lambda i, j: (i, 0)),
            pl.BlockSpec((1, d), lambda i, j: (0, 0)),
            pl.BlockSpec((None, 1, d), lambda i, j: (i * tm // seq, 0, 0)),
            pl.BlockSpec((None, 1, d), lambda i, j: (i * tm // seq, 0, 1)),
            pl.BlockSpec((d, INPROJ_COLS), lambda i, j: (0, j)),
        ],
        out_specs=pl.BlockSpec((tm, INPROJ_COLS), lambda i, j: (i, j)),
        compiler_params=pltpu.CompilerParams(dimension_semantics=("arbitrary", "arbitrary"),
                                             vmem_limit_bytes=VMEM_LIMIT),
        name="inproj",
    )(x2d, norm_g, ada3, ada3, w_in_p)


ATT_GROUP_MAX = 5
DEINT = 4


def _group_size(n):
    return max(g for g in range(1, ATT_GROUP_MAX + 1) if n % g == 0)


def _aligned(row):
    return row if isinstance(row, int) else pl.multiple_of(row, ATT_BLOCK)


def _mixc_kernel(q_ref, k_ref, v_ref, bias_ref, o_ref, nat32, d4f, d4b, d16b, o_st, lse_st):
    seq = q_ref.shape[0]
    ncfg = len(C_CONFIGS)
    assert [dil for _, dil in C_CONFIGS] == [1, DEINT, DEINT * DEINT]
    len4 = seq // DEINT
    len16 = len4 // DEINT
    for a, ref in enumerate((q_ref, k_ref, v_ref)):
        nat32[a] = ref[...].astype(F32)
    for a in range(3):
        for r in range(DEINT):
            part = nat32[a, pl.ds(r, len4, stride=DEINT), :]
            d4f[a, r * len4:(r + 1) * len4, :] = part
            d4b[a, r * len4:(r + 1) * len4, :] = part.astype(BF16)
    for a in range(3):
        for d in range(DEINT * DEINT):
            part = d4f[a, pl.ds((d % DEINT) * len4 + d // DEINT, len16, stride=DEINT), :]
            d16b[a, d * len16:(d + 1) * len16, :] = part.astype(BF16)
    sources = ((q_ref, k_ref, v_ref), tuple(d4b.at[a] for a in range(3)), tuple(d16b.at[a] for a in range(3)))
    low_half = lax.broadcasted_iota(jnp.int32, (ATT_BLOCK, LANES), 1) < HEAD_DIM

    def tiles(ci, dil, where, with_prev):
        q_src, k_src, v_src = sources[ci]
        class_len = seq // dil
        nkeys = 2 * ATT_BLOCK if with_prev else ATT_BLOCK

        def state_rows(res, blk):
            start = blk * (ATT_BLOCK * dil) + res
            if dil > 1:
                return pl.ds(start, ATT_BLOCK, stride=dil)
            return pl.ds(_aligned(start), ATT_BLOCK)

        bias = bias_ref[ci] if with_prev else bias_ref[ci, :, ATT_BLOCK:]
        operands = []
        for res, blk in where:
            off = res * class_len + blk * ATT_BLOCK
            q = q_src[pl.ds(_aligned(off), ATT_BLOCK), :]
            zero = jnp.zeros_like(q)
            lhs = jnp.concatenate([jnp.where(low_half, q, zero), jnp.where(low_half, zero, q)], axis=0)
            key_rows = pl.ds(_aligned(off + ATT_BLOCK - nkeys), nkeys)
            kk = k_src[key_rows, :]
            vv = v_src[key_rows, :]
            operands.append((lhs, kk, jnp.concatenate([vv, jnp.ones_like(vv)], axis=1)))
        logits = [lax.dot_general(lhs, kk, (((1,), (1,)), ((), ())), preferred_element_type=F32)
                  for lhs, kk, _ in operands]
        probs = []
        for s in logits:
            s = s + bias
            m = jnp.max(s, axis=-1, keepdims=True)
            probs.append((m, jnp.exp2(s - m).astype(BF16)))
        results = [jnp.dot(p, v_aug, preferred_element_type=F32)
                   for (_, p), (_, _, v_aug) in zip(probs, operands)]
        for (res, blk), (m, _), r in zip(where, probs, results):
            rows = state_rows(res, blk)
            denom = jnp.where(low_half, r[:ATT_BLOCK, LANES:], r[ATT_BLOCK:, LANES:])
            o_st[ci, rows, :] = jnp.where(low_half, r[:ATT_BLOCK, :LANES], r[ATT_BLOCK:, :LANES]) / denom
            lse_st[ci, rows, :] = jnp.where(low_half, m[:ATT_BLOCK], m[ATT_BLOCK:]) + jnp.log2(denom)

    for ci, (_, dil) in enumerate(C_CONFIGS):
        n_blk = seq // (dil * ATT_BLOCK)
        first = _group_size(dil)

        def first_body(g, carry, ci=ci, dil=dil, first=first):
            tiles(ci, dil, [(g * first + u, 0) for u in range(first)], with_prev=False)
            return carry

        if dil == first:
            first_body(0, 0)
        else:
            lax.fori_loop(0, dil // first, first_body, 0)
        n_inner = (n_blk - 1) * dil
        if n_inner == 0:
            continue
        group = _group_size(n_inner)

        def body(g, carry, ci=ci, dil=dil, group=group):
            where = []
            for u in range(group):
                t = dil + g * group + u
                blk = t // dil
                where.append((t - blk * dil, blk))
            tiles(ci, dil, where, with_prev=True)
            return carry

        lax.fori_loop(0, n_inner // group, body, 0)

    lse = [lse_st[ci] for ci in range(ncfg)]
    top = functools.reduce(jnp.maximum, lse)
    num = jnp.zeros((seq, LANES), F32)
    den = jnp.zeros((seq, LANES), F32)
    for ci in range(ncfg):
        w = jnp.exp2(lse[ci] - top)
        num = num + w * o_st[ci]
        den = den + w
    o_ref[...] = (num / den).astype(BF16)


def _mixc(z, bias, bsz, seq):
    n = z.shape[0]
    npairs = WIDTH // LANES
    ncfg = len(C_CONFIGS)
    bias = bias.reshape(ncfg, npairs, 2 * ATT_BLOCK, 2 * ATT_BLOCK)
    qkv_spec = lambda zb: pl.BlockSpec((seq, LANES), lambda p, b: (b, zb * npairs + p))
    return pl.pallas_call(
        _mixc_kernel,
        out_shape=jax.ShapeDtypeStruct((n, WIDTH), BF16),
        grid=(npairs, bsz),
        in_specs=[
            qkv_spec(ZB_Q), qkv_spec(ZB_K), qkv_spec(ZB_V),
            pl.BlockSpec((ncfg, None, 2 * ATT_BLOCK, 2 * ATT_BLOCK), lambda p, b: (0, p, 0, 0)),
        ],
        out_specs=pl.BlockSpec((seq, LANES), lambda p, b: (b, p)),
        scratch_shapes=[
            pltpu.VMEM((3, seq, LANES), F32),
            pltpu.VMEM((3, seq, LANES), F32),
            pltpu.VMEM((3, seq, LANES), BF16),
            pltpu.VMEM((3, seq, LANES), BF16),
            pltpu.VMEM((ncfg, seq, LANES), F32),
            pltpu.VMEM((ncfg, seq, LANES), F32),
        ],
        compiler_params=pltpu.CompilerParams(dimension_semantics=("arbitrary", "arbitrary"),
                                             vmem_limit_bytes=VMEM_LIMIT),
        name="mixer_c",
    )(z, z, z, bias)


def _gmlp(u_ref, v_ref, ln_ref, ws_ref, bs_ref):
    tile = u_ref.shape[0]
    row = lax.broadcasted_iota(jnp.int32, (A_CHUNK, A_CHUNK), 0)
    col = lax.broadcasted_iota(jnp.int32, (A_CHUNK, A_CHUNK), 1)
    causal = row >= col
    low_half = lax.broadcasted_iota(jnp.int32, (A_CHUNK, LANES), 1) < HEAD_DIM
    weights = [jnp.where(causal, ws_ref[g], 0) for g in range(N_HEADS)]
    chunks = []
    for c in range(tile // A_CHUNK):
        rows = pl.ds(c * A_CHUNK, A_CHUNK)
        v = v_ref[rows, :].astype(F32)
        mu = jnp.mean(v, axis=-1, keepdims=True)
        var = jnp.mean(jnp.square(v - mu), axis=-1, keepdims=True)
        vn = ((v - mu) * lax.rsqrt(var + EPS) * ln_ref[...]).astype(BF16)
        parts = []
        for p in range(WIDTH // LANES):
            vp = vn[:, p * LANES:(p + 1) * LANES]
            r0 = jnp.dot(weights[2 * p], vp, preferred_element_type=F32)
            r1 = jnp.dot(weights[2 * p + 1], vp, preferred_element_type=F32)
            parts.append(jnp.where(low_half, r0, r1))
        sv = jnp.concatenate(parts, axis=-1) + bs_ref[...]
        chunks.append((u_ref[rows, :].astype(F32) * sv).astype(BF16))
    return jnp.concatenate(chunks, axis=0)


def _rglru(x_ref, g_ref, cw_ref, cb_ref, wax_ref, ba_ref, bx_ref, lam_ref, xs_ref, h_ref, first):
    tile = x_ref.shape[0]
    x = x_ref[...].astype(F32)
    xs_ref[SUBLANES:SUBLANES + tile, :] = x
    xb = cb_ref[...] + cw_ref[B_CONV - 1:B_CONV, :] * x
    for k in range(B_CONV - 1):
        back = B_CONV - 1 - k
        xb = xb + cw_ref[k:k + 1, :] * xs_ref[pl.ds(SUBLANES - back, tile), :]
    xs_ref[0:SUBLANES, :] = xs_ref[tile:tile + SUBLANES, :]

    xbb = xb.astype(BF16)
    r_parts, i_parts = [], []
    for blk in range(WIDTH // MXU_DIM):
        ra = jnp.dot(xbb[:, blk * MXU_DIM:(blk + 1) * MXU_DIM], wax_ref[blk],
                     preferred_element_type=F32)
        r_parts.append(ra[:, :MXU_DIM])
        i_parts.append(ra[:, MXU_DIM:])
    r = jax.nn.sigmoid(jnp.concatenate(r_parts, axis=-1) + ba_ref[...])
    gate_i = jax.nn.sigmoid(jnp.concatenate(i_parts, axis=-1) + bx_ref[...])
    neg_lam = -lam_ref[...]
    softplus = jnp.maximum(neg_lam, 0.0) + jnp.log1p(jnp.exp(-jnp.abs(neg_lam)))
    a = jnp.exp2((-RG_C * LOG2E * softplus) * r)
    y = 1.0 - a * a
    mult = y * lax.rsqrt(jnp.maximum(y, 1e-30))
    row = lax.broadcasted_iota(jnp.int32, (tile, WIDTH), 0)
    mult = jnp.where(jnp.logical_and(first, row == 0), 1.0, mult)
    b = xb * gate_i * mult

    srow = lax.broadcasted_iota(jnp.int32, (SUBLANES, WIDTH), 0)
    h = h_ref[...]
    hs = []
    for gi in range(tile // SUBLANES):
        av = a[gi * SUBLANES:(gi + 1) * SUBLANES, :]
        bv = b[gi * SUBLANES:(gi + 1) * SUBLANES, :]
        for d in (1, 2, 4):
            keep = srow >= d
            a_sh = jnp.where(keep, pltpu.roll(av, d, 0), 1.0)
            b_sh = jnp.where(keep, pltpu.roll(bv, d, 0), 0.0)
            bv = av * b_sh + bv
            av = av * a_sh
        hv = av * h + bv
        hs.append(hv)
        h = jnp.broadcast_to(hv[SUBLANES - 1:SUBLANES, :], (SUBLANES, WIDTH))
    h_ref[...] = h
    return (jnp.concatenate(hs, axis=0) * g_ref[...].astype(F32)).astype(BF16)


def _mix_merge_kernel(x_ref, au_ref, av_ref, bg_ref, bx_ref, oc_ref, ga_ref, gb_ref, gc_ref, g1_ref,
                      ln_ref, ws_ref, bs_ref, cw_ref, cb_ref, wax_ref, ba_ref, bxb_ref, lam_ref,
                      pa_ref, pb_ref, pc_ref, wo_ref, o_ref, xs_ref, h_ref):
    first = pl.program_id(1) == 0

    @pl.when(first)
    def _():
        xs_ref[0:SUBLANES, :] = jnp.zeros((SUBLANES, WIDTH), F32)
        h_ref[...] = jnp.zeros(h_ref.shape, F32)

    out_a = _gmlp(au_ref, av_ref, ln_ref, ws_ref, bs_ref)
    out_b = _rglru(bx_ref, bg_ref, cw_ref, cb_ref, wax_ref, ba_ref, bxb_ref, lam_ref, xs_ref, h_ref, first)
    merged = None
    for out, g_r, p_r in ((out_a, ga_ref, pa_ref), (out_b, gb_ref, pb_ref), (oc_ref[...], gc_ref, pc_ref)):
        term = g_r[...].astype(F32) * jnp.dot(out, p_r[...], preferred_element_type=F32)
        merged = term if merged is None else merged + term
    y = jnp.dot(merged.astype(BF16), wo_ref[...], preferred_element_type=F32)
    o_ref[...] = x_ref[...] + g1_ref[...] * y


def _mix_merge(x2d, z, out_c, ada3, a_ln, a_ws, bs_full, conv_w, conv_b, wax, ba, bx, lam,
               p_a, p_b, p_c, w_out, bsz, seq, tile):
    n, d = x2d.shape
    nt = seq // tile
    rows = lambda b, j: b * nt + j
    branch = lambda zb: pl.BlockSpec((tile, WIDTH), lambda b, j: (rows(b, j), zb))
    gate = lambda k: pl.BlockSpec((tile, d), lambda b, j: (rows(b, j), k))
    whole = lambda *shape: pl.BlockSpec(shape, lambda b, j: (0,) * len(shape))
    return pl.pallas_call(
        _mix_merge_kernel,
        out_shape=jax.ShapeDtypeStruct((n, d), F32),
        grid=(bsz, nt),
        in_specs=[
            pl.BlockSpec((tile, d), lambda b, j: (rows(b, j), 0)),
            branch(ZB_AU), branch(ZB_AV), branch(ZB_BG), branch(ZB_BX),
            pl.BlockSpec((tile, WIDTH), lambda b, j: (rows(b, j), 0)),
            gate(0), gate(1), gate(2),
            pl.BlockSpec((None, 1, d), lambda b, j: (b, 0, 2)),
            whole(1, WIDTH), whole(N_HEADS, A_CHUNK, A_CHUNK), whole(A_CHUNK, WIDTH),
            whole(B_CONV, WIDTH), whole(1, WIDTH), whole(WIDTH // MXU_DIM, MXU_DIM, 2 * MXU_DIM),
            whole(1, WIDTH), whole(1, WIDTH), whole(1, WIDTH),
            whole(WIDTH, d), whole(WIDTH, d), whole(WIDTH, d), whole(d, d),
        ],
        out_specs=pl.BlockSpec((tile, d), lambda b, j: (rows(b, j), 0)),
        scratch_shapes=[
            pltpu.VMEM((tile + SUBLANES, WIDTH), F32),
            pltpu.VMEM((SUBLANES, WIDTH), F32),
        ],
        input_output_aliases={0: 0},
        compiler_params=pltpu.CompilerParams(dimension_semantics=("arbitrary", "arbitrary"),
                                             vmem_limit_bytes=VMEM_LIMIT),
        name="mix_merge",
    )(x2d, z, z, z, z, out_c, z, z, z, ada3, a_ln, a_ws, bs_full, conv_w, conv_b, wax, ba, bx, lam,
      p_a, p_b, p_c, w_out)


FFN_HALO = BF16_ROWS
FFN_FBLOCK = MXU_DIM


def _ffn_kernel(x_ref, xh_ref, g_ref, sh_ref, sc_ref, g2_ref, wg_ref, wu_ref, cw_ref, cb_ref, wd_ref,
                fin_ref, o_ref, h_ref, *, tiles_per_seq, final):
    x = x_ref[...]
    xn = _rms_mod(x, g_ref[...], sh_ref[...], sc_ref[...]).astype(BF16)
    seq_start = pl.program_id(0) % tiles_per_seq == 0
    xh = _rms_mod(xh_ref[...], g_ref[...], sh_ref[...], sc_ref[...])
    xh = jnp.where(seq_start, 0.0, xh).astype(BF16)
    xe = jnp.concatenate([xh, xn], axis=0)
    for f in range(0, D_FF, FFN_FBLOCK):
        cols = slice(f, f + FFN_FBLOCK)
        ge = jnp.dot(xe, wg_ref[:, cols], preferred_element_type=F32)
        up = jnp.dot(xn, wu_ref[:, cols], preferred_element_type=F32)
        gc = cb_ref[:, cols] + cw_ref[FFN_CONV - 1:FFN_CONV, cols] * ge[FFN_HALO:, :]
        for k in range(FFN_CONV - 1):
            back = FFN_CONV - 1 - k
            gc = gc + cw_ref[k:k + 1, cols] * pltpu.roll(ge, back, 0)[FFN_HALO:, :]
        h_ref[:, cols] = (_gelu(gc) * up).astype(BF16)
    y = jnp.dot(h_ref[...], wd_ref[...], preferred_element_type=F32)
    xo = x + g2_ref[...] * y
    if final:
        xo = xo * lax.rsqrt(jnp.mean(xo * xo, axis=-1, keepdims=True) + EPS) * fin_ref[...]
    o_ref[...] = xo


def _ffn(x2d, norm_g, ada3, w_gate, w_up, conv_w, conv_b, w_down, final_g, seq, tm, final):
    n, d = x2d.shape
    tiles_per_seq = seq // tm
    halo_blocks = tm // FFN_HALO
    vec = pl.BlockSpec((1, d), lambda i: (0, 0))
    ada = lambda k: pl.BlockSpec((None, 1, d), lambda i: (i * tm // seq, 0, k))
    resident = dict(pipeline_mode=pl.Buffered(1))
    return pl.pallas_call(
        functools.partial(_ffn_kernel, tiles_per_seq=tiles_per_seq, final=final),
        out_shape=jax.ShapeDtypeStruct((n, d), F32),
        grid=(n // tm,),
        in_specs=[
            pl.BlockSpec((tm, d), lambda i: (i, 0)),
            pl.BlockSpec((FFN_HALO, d), lambda i: (jnp.maximum(i * halo_blocks - 1, 0), 0)),
            vec, ada(3), ada(4), ada(5),
            pl.BlockSpec((d, D_FF), lambda i: (0, 0), **resident),
            pl.BlockSpec((d, D_FF), lambda i: (0, 0), **resident),
            pl.BlockSpec((FFN_CONV, D_FF), lambda i: (0, 0)),
            pl.BlockSpec((1, D_FF), lambda i: (0, 0)),
            pl.BlockSpec((D_FF, d), lambda i: (0, 0), **resident),
            vec,
        ],
        out_specs=pl.BlockSpec((tm, d), lambda i: (i, 0)),
        scratch_shapes=[pltpu.VMEM((tm, D_FF), BF16)],
        compiler_params=pltpu.CompilerParams(dimension_semantics=("arbitrary",),
                                             vmem_limit_bytes=VMEM_LIMIT),
        name="ffn",
    )(x2d, x2d, norm_g, ada3, ada3, ada3, w_gate, w_up, conv_w, conv_b, w_down, final_g)


def _block_diag(w, per):
    depth, nb, r, c = w.shape
    w = w.reshape(depth, nb // per, per, r, c)
    eye = jnp.eye(per, dtype=w.dtype)
    return jnp.einsum("dbirc,ij->dbirjc", w, eye).reshape(depth, nb // per, per * r, per * c)


def kernel(x, c, w_ada, b_ada, norm1, w_in, a_ln, a_ws, a_bs, b_conv_w, b_conv_b, b_wa, b_ba, b_wx, b_bx,
           b_lam, rel_table, p_a, p_b, p_c, w_out, norm2, f_wgate, f_wup, f_conv_w, f_conv_b, f_wdown,
           final_norm):
    bsz, seq, d = x.shape
    depth = w_in.shape[0]
    n = bsz * seq

    per = MXU_DIM // HEAD_DIM
    w_in_p = jnp.concatenate(
        [w_in[..., _W_IN_OFFSET[m]:_W_IN_OFFSET[m] + _Z_WIDTH[m]] for m in _Z_ORDER], axis=-1).astype(BF16)
    wax = jnp.concatenate([_block_diag(b_wa, per), _block_diag(b_wx, per)], axis=-1).astype(BF16)
    a_ws_b = a_ws.astype(BF16)
    bs_full = jnp.repeat(jnp.swapaxes(a_bs, 1, 2), HEAD_DIM, axis=2)
    p_a_b, p_b_b, p_c_b, w_out_b = (w.astype(BF16) for w in (p_a, p_b, p_c, w_out))
    wg_b, wu_b, wd_b = (w.astype(BF16) for w in (f_wgate, f_wup, f_wdown))

    ada = _ada(c, w_ada, b_ada)
    bias = _bias_tables(rel_table)

    x2d = x.reshape(n, d)
    row = lambda v: v.reshape(1, -1)
    for l in range(depth):
        ada3 = ada[l].reshape(bsz, 1, 6 * d)
        z = _inproj(x2d, row(norm1[l]), ada3, w_in_p[l], seq, tm=1024)
        out_c = _mixc(z, bias, bsz, seq)
        x2d = _mix_merge(x2d, z, out_c, ada3, row(a_ln[l]), a_ws_b[l], bs_full[l], b_conv_w[l],
                         row(b_conv_b[l]), wax[l], row(b_ba[l]), row(b_bx[l]), row(b_lam[l]),
                         p_a_b[l], p_b_b[l], p_c_b[l], w_out_b[l], bsz, seq, tile=512)
        x2d = _ffn(x2d, row(norm2[l]), ada3, wg_b[l], wu_b[l], f_conv_w[l], row(f_conv_b[l]), wd_b[l],
                   row(final_norm), seq, tm=512, final=(l == depth - 1))
    return x2d.reshape(bsz, seq, d)
```

```python
import functools
import math

import numpy as np
import jax
import jax.numpy as jnp
from jax import lax
from jax.experimental import pallas as pl
from jax.experimental.pallas import tpu as pltpu

F32 = jnp.float32
BF16 = jnp.bfloat16

D_MODEL = 1024
HEAD_DIM = 64
WIDTH = 768
N_HEADS = WIDTH // HEAD_DIM
A_CHUNK = 128
B_CONV = 4
RG_C = 8.0
C_CONFIGS = ((128, 1), (512, 4), (2048, 16))
ATT_BLOCK = 128
N_BUCKETS = 32
MAX_DISTANCE = 2048
N_BRANCH = 3
D_FF = 2816
FFN_CONV = 3
EPS = 1e-6
NEG_INF = -1e30
GATE_COLS = N_BRANCH * D_MODEL
IN_COLS = 7 * WIDTH + GATE_COLS
LOG2E = math.log2(math.e)
Q_SCALE = LOG2E * HEAD_DIM ** -0.5

LANES = 128
SUBLANES = 8
BF16_ROWS = 16
MXU_DIM = 256

_Z_ORDER = ("gates", "a_u", "a_v", "b_g", "b_x", "q", "k", "v")
_W_IN_OFFSET = {"a_u": 0, "a_v": WIDTH, "b_x": 2 * WIDTH, "b_g": 3 * WIDTH, "q": 4 * WIDTH, "k": 5 * WIDTH,
                "v": 6 * WIDTH, "gates": 7 * WIDTH}
_Z_ACT = {"gates": "sigmoid", "a_u": "gelu", "a_v": "gelu", "b_g": "gelu", "b_x": "id", "q": "qscale",
          "k": "id", "v": "id"}
_Z_WIDTH = {name: (GATE_COLS if name == "gates" else WIDTH) for name in _Z_ORDER}
_Z_START = {name: sum(_Z_WIDTH[m] for m in _Z_ORDER[:i]) for i, name in enumerate(_Z_ORDER)}
ZB_AU, ZB_AV, ZB_BG, ZB_BX, ZB_Q, ZB_K, ZB_V = (_Z_START[m] // WIDTH for m in _Z_ORDER[1:])

VMEM_LIMIT = 56 * 1024 * 1024


def _gelu(x):
    return jax.nn.gelu(x)


def _rms_mod(x, g, shift, scale):
    y = x * lax.rsqrt(jnp.mean(x * x, axis=-1, keepdims=True) + EPS) * g
    return y * (1.0 + scale) + shift


def _ada_kernel(c_ref, w_ref, b_ref, o_ref):
    c = c_ref[...]
    cond = c * jax.nn.sigmoid(c)
    o_ref[...] = jnp.dot(cond, w_ref[...], preferred_element_type=F32,
                         precision=lax.Precision.HIGHEST) + b_ref[...]


def _ada(c, w_ada, b_ada):
    depth, d, n = w_ada.shape
    bsz = c.shape[0]
    return pl.pallas_call(
        _ada_kernel,
        out_shape=jax.ShapeDtypeStruct((depth, bsz, n), F32),
        grid=(depth, n // D_MODEL),
        in_specs=[
            pl.BlockSpec((bsz, d), lambda l, j: (0, 0)),
            pl.BlockSpec((None, d, D_MODEL), lambda l, j: (l, 0, j)),
            pl.BlockSpec((None, 1, D_MODEL), lambda l, j: (l, 0, j)),
        ],
        out_specs=pl.BlockSpec((None, bsz, D_MODEL), lambda l, j: (l, 0, j)),
        compiler_params=pltpu.CompilerParams(dimension_semantics=("arbitrary", "arbitrary")),
        name="ada",
    )(c, w_ada, b_ada.reshape(depth, 1, n))


def _t5_bucket(dist):
    max_exact = N_BUCKETS // 2
    d = np.maximum(dist, 1).astype(np.float32)
    large = max_exact + (np.log(d / max_exact) / np.log(MAX_DISTANCE / max_exact)
                         * (N_BUCKETS - max_exact)).astype(np.int32)
    large = np.minimum(large, N_BUCKETS - 1)
    return np.where(dist < max_exact, dist, large).astype(np.int32)


def _bias_tables_static():
    qi = np.arange(ATT_BLOCK)[:, None]
    kk = np.arange(2 * ATT_BLOCK)[None, :]
    dist = qi + ATT_BLOCK - kk
    tables = []
    for window, dil in C_CONFIGS:
        band = (dist >= 0) & (dist <= window // dil)
        tables.append(np.where(band, _t5_bucket(np.maximum(dist, 0) * dil), -1))
    return np.stack(tables).astype(np.int32)


def _bias_kernel(rel_ref, bucket_ref, o_ref):
    h = pl.program_id(1)
    bk = bucket_ref[...]
    acc = jnp.full(bk.shape, NEG_INF, F32)
    for b in range(N_BUCKETS):
        acc = jnp.where(bk == b, rel_ref[b, h] * LOG2E, acc)
    o_ref[...] = acc


def _bias_tables(rel_table):
    buckets = _bias_tables_static()
    ncfg = len(C_CONFIGS)
    blk = (ATT_BLOCK, 2 * ATT_BLOCK)
    return pl.pallas_call(
        _bias_kernel,
        out_shape=jax.ShapeDtypeStruct((ncfg, N_HEADS) + blk, F32),
        grid=(ncfg, N_HEADS),
        in_specs=[
            pl.BlockSpec(memory_space=pltpu.SMEM),
            pl.BlockSpec((None,) + blk, lambda c, h: (c, 0, 0)),
        ],
        out_specs=pl.BlockSpec((None, None) + blk, lambda c, h: (c, h, 0, 0)),
        compiler_params=pltpu.CompilerParams(dimension_semantics=("arbitrary",) * 2),
        name="bias_tables",
    )(rel_table, jnp.asarray(buckets))


INPROJ_BLOCKS = 3
INPROJ_COLS = IN_COLS // INPROJ_BLOCKS


def _inproj_segments(block):
    lo, hi = block * INPROJ_COLS, (block + 1) * INPROJ_COLS
    segs = []
    for name in _Z_ORDER:
        a, b = max(_Z_START[name], lo), min(_Z_START[name] + _Z_WIDTH[name], hi)
        if a < b:
            if segs and segs[-1][2] == _Z_ACT[name]:
                segs[-1] = (segs[-1][0], b - lo, _Z_ACT[name])
            else:
                segs.append((a - lo, b - lo, _Z_ACT[name]))
    return segs


def _inproj_kernel(x_ref, g_ref, sh_ref, sc_ref, w_ref, z_ref):
    acts = {"sigmoid": jax.nn.sigmoid, "gelu": _gelu, "id": lambda t: t, "qscale": lambda t: t * Q_SCALE}
    for block in range(INPROJ_BLOCKS):
        @pl.when(pl.program_id(1) == block)
        def _(block=block):
            xn = _rms_mod(x_ref[...], g_ref[...], sh_ref[...], sc_ref[...]).astype(BF16)
            for a, b, act in _inproj_segments(block):
                acc = jnp.dot(xn, w_ref[:, a:b], preferred_element_type=F32)
                z_ref[:, a:b] = acts[act](acc).astype(BF16)


def _inproj(x2d, norm_g, ada3, w_in_p, seq, tm):
    n, d = x2d.shape
    return pl.pallas_call(
        _inproj_kernel,
        out_shape=jax.ShapeDtypeStruct((n, IN_COLS), BF16),
        grid=(n // tm, INPROJ_BLOCKS),
        in_specs=[
            pl.BlockSpec((tm, d), lambda i, j: (i, 0)),
            pl.BlockSpec((1, d), lambda i, j: (0, 0)),
            pl.BlockSpec((None, 1, d), lambda i, j: (i * tm // seq, 0, 0)),
            pl.BlockSpec((None, 1, d), lambda i, j: (i * tm // seq, 0, 1)),
            pl.BlockSpec((d, INPROJ_COLS), lambda i, j: (0, j)),
        ],
        out_specs=pl.BlockSpec((tm, INPROJ_COLS), lambda i, j: (i, j)),
        compiler_params=pltpu.CompilerParams(dimension_semantics=("arbitrary", "arbitrary"),
                                             vmem_limit_bytes=VMEM_LIMIT),
        name="inproj",
    )(x2d, norm_g, ada3, ada3, w_in_p)


ATT_GROUP = 4
ATT_TILES = 16
DEINT = 4
Q_ROW, K_ROW, V_ROW = 0, 1, 2


def _mixc_kernel(q_ref, k_ref, v_ref, bias_ref, o_ref, src, nat32, d4f, p_buf, m_buf, o_st, lse_st, tmp):
    seq = q_ref.shape[0]
    ncfg = len(C_CONFIGS)
    assert [dil for _, dil in C_CONFIGS] == [1, DEINT, DEINT * DEINT]
    assert seq == ATT_TILES * ATT_BLOCK and ATT_TILES == DEINT * DEINT and (ncfg * ATT_TILES) % ATT_GROUP == 0
    len4 = seq // DEINT

    for a, ref in enumerate((q_ref, k_ref, v_ref)):
        val = ref[...]
        src[0, a] = val
        nat32[a] = val.astype(F32)
    for a in range(3):
        for r in range(DEINT):
            part = nat32[a, pl.ds(r, len4, stride=DEINT), :]
            d4f[a, r * len4:(r + 1) * len4, :] = part
            src[1, a, r * len4:(r + 1) * len4, :] = part.astype(BF16)
    for a in range(3):
        for d in range(ATT_TILES):
            part = d4f[a, pl.ds((d % DEINT) * len4 + d // DEINT, ATT_BLOCK, stride=DEINT), :]
            src[2, a, d * ATT_BLOCK:(d + 1) * ATT_BLOCK, :] = part.astype(BF16)
    low_half = lax.broadcasted_iota(jnp.int32, (ATT_BLOCK, LANES), 1) < HEAD_DIM

    def coords(t):
        ci, j = divmod(t, ATT_TILES)
        dil = C_CONFIGS[ci][1]
        blk, res = divmod(j, dil)
        return ci, res * (seq // dil) + blk * ATT_BLOCK, (ATT_BLOCK if blk == 0 else 2 * ATT_BLOCK)

    def scores(g, slot):
        for u in range(ATT_GROUP):
            ci, off, nkeys = coords(g * ATT_GROUP + u)
            q = src[ci, Q_ROW, off:off + ATT_BLOCK, :]
            zero = jnp.zeros_like(q)
            lhs = jnp.concatenate([jnp.where(low_half, q, zero), jnp.where(low_half, zero, q)], axis=0)
            kk = src[ci, K_ROW, off + ATT_BLOCK - nkeys:off + ATT_BLOCK, :]
            s = lax.dot_general(lhs, kk, (((1,), (1,)), ((), ())), preferred_element_type=F32)
            s = s + bias_ref[ci, :, 2 * ATT_BLOCK - nkeys:]
            m = jnp.max(s, axis=-1, keepdims=True)
            p_buf[slot, u, :, :nkeys] = jnp.exp2(s - m).astype(BF16)
            m_buf[slot, u] = jnp.where(low_half, m[:ATT_BLOCK], m[ATT_BLOCK:])

    def values(g, slot):
        for u in range(ATT_GROUP):
            ci, off, nkeys = coords(g * ATT_GROUP + u)
            vv = src[ci, V_ROW, off + ATT_BLOCK - nkeys:off + ATT_BLOCK, :]
            v_aug = jnp.concatenate([vv, jnp.ones_like(vv)], axis=1)
            r = jnp.dot(p_buf[slot, u, :, :nkeys], v_aug, preferred_element_type=F32)
            denom = jnp.where(low_half, r[:ATT_BLOCK, LANES:], r[ATT_BLOCK:, LANES:])
            rows = slice(off, off + ATT_BLOCK)
            o_st[ci, rows, :] = jnp.where(low_half, r[:ATT_BLOCK, :LANES], r[ATT_BLOCK:, :LANES]) / denom
            lse_st[ci, rows, :] = m_buf[slot, u] + jnp.log2(denom)

    n_groups = ncfg * ATT_TILES // ATT_GROUP
    scores(0, 0)
    for k in range(1, n_groups):
        values(k - 1, (k - 1) % 2)
        scores(k, k % 2)
    values(n_groups - 1, (n_groups - 1) % 2)

    for d in range(ATT_TILES):
        rows4 = pl.ds((d % DEINT) * len4 + d // DEINT, ATT_BLOCK, stride=DEINT)
        tmp[0, rows4, :] = o_st[2, d * ATT_BLOCK:(d + 1) * ATT_BLOCK, :]
        tmp[1, rows4, :] = lse_st[2, d * ATT_BLOCK:(d + 1) * ATT_BLOCK, :]
    for r in range(DEINT):
        nat_rows = pl.ds(r, len4, stride=DEINT)
        cls_rows = pl.ds(r * len4, len4)
        outs = (o_st[0, nat_rows, :], o_st[1, cls_rows, :], tmp[0, cls_rows, :])
        lses = (lse_st[0, nat_rows, :], lse_st[1, cls_rows, :], tmp[1, cls_rows, :])
        top = functools.reduce(jnp.maximum, lses)
        weights = [jnp.exp2(lse - top) for lse in lses]
        num = sum(w * o for w, o in zip(weights, outs))
        nat32[0, nat_rows, :] = num / sum(weights)
    o_ref[...] = nat32[0].astype(BF16)


def _mixc(z, bias, bsz, seq):
    n = z.shape[0]
    npairs = WIDTH // LANES
    ncfg = len(C_CONFIGS)
    bias = bias.reshape(ncfg, npairs, 2 * ATT_BLOCK, 2 * ATT_BLOCK)
    qkv_spec = lambda zb: pl.BlockSpec((seq, LANES), lambda p, b: (b, zb * npairs + p))
    return pl.pallas_call(
        _mixc_kernel,
        out_shape=jax.ShapeDtypeStruct((n, WIDTH), BF16),
        grid=(npairs, bsz),
        in_specs=[
            qkv_spec(ZB_Q), qkv_spec(ZB_K), qkv_spec(ZB_V),
            pl.BlockSpec((ncfg, None, 2 * ATT_BLOCK, 2 * ATT_BLOCK), lambda p, b: (0, p, 0, 0)),
        ],
        out_specs=pl.BlockSpec((seq, LANES), lambda p, b: (b, p)),
        scratch_shapes=[
            pltpu.VMEM((ncfg, 3, seq, LANES), BF16),
            pltpu.VMEM((3, seq, LANES), F32),
            pltpu.VMEM((3, seq, LANES), F32),
            pltpu.VMEM((2, ATT_GROUP, 2 * ATT_BLOCK, 2 * ATT_BLOCK), BF16),
            pltpu.VMEM((2, ATT_GROUP, ATT_BLOCK, LANES), F32),
            pltpu.VMEM((ncfg, seq, LANES), F32),
            pltpu.VMEM((ncfg, seq, LANES), F32),
            pltpu.VMEM((2, seq, LANES), F32),
        ],
        compiler_params=pltpu.CompilerParams(dimension_semantics=("arbitrary", "arbitrary"),
                                             vmem_limit_bytes=VMEM_LIMIT),
        name="mixer_c",
    )(z, z, z, bias)


def _gmlp(u_ref, v_ref, ln_ref, ws_ref, bs_ref):
    tile = u_ref.shape[0]
    row = lax.broadcasted_iota(jnp.int32, (A_CHUNK, A_CHUNK), 0)
    col = lax.broadcasted_iota(jnp.int32, (A_CHUNK, A_CHUNK), 1)
    causal = row >= col
    low_half = lax.broadcasted_iota(jnp.int32, (A_CHUNK, LANES), 1) < HEAD_DIM
    weights = [jnp.where(causal, ws_ref[g], 0) for g in range(N_HEADS)]
    chunks = []
    for c in range(tile // A_CHUNK):
        rows = pl.ds(c * A_CHUNK, A_CHUNK)
        v = v_ref[rows, :].astype(F32)
        mu = jnp.mean(v, axis=-1, keepdims=True)
        var = jnp.mean(jnp.square(v - mu), axis=-1, keepdims=True)
        vn = ((v - mu) * lax.rsqrt(var + EPS) * ln_ref[...]).astype(BF16)
        parts = []
        for p in range(WIDTH // LANES):
            vp = vn[:, p * LANES:(p + 1) * LANES]
            r0 = jnp.dot(weights[2 * p], vp, preferred_element_type=F32)
            r1 = jnp.dot(weights[2 * p + 1], vp, preferred_element_type=F32)
            parts.append(jnp.where(low_half, r0, r1))
        sv = jnp.concatenate(parts, axis=-1) + bs_ref[...]
        chunks.append((u_ref[rows, :].astype(F32) * sv).astype(BF16))
    return jnp.concatenate(chunks, axis=0)


def _rglru(x_ref, g_ref, cw_ref, cb_ref, wax_ref, ba_ref, bx_ref, lam_ref, xs_ref, h_ref, first):
    tile = x_ref.shape[0]
    x = x_ref[...].astype(F32)
    xs_ref[SUBLANES:SUBLANES + tile, :] = x
    xb = cb_ref[...] + cw_ref[B_CONV - 1:B_CONV, :] * x
    for k in range(B_CONV - 1):
        back = B_CONV - 1 - k
        xb = xb + cw_ref[k:k + 1, :] * xs_ref[pl.ds(SUBLANES - back, tile), :]
    xs_ref[0:SUBLANES, :] = xs_ref[tile:tile + SUBLANES, :]

    xbb = xb.astype(BF16)
    r_parts, i_parts = [], []
    for blk in range(WIDTH // MXU_DIM):
        ra = jnp.dot(xbb[:, blk * MXU_DIM:(blk + 1) * MXU_DIM], wax_ref[blk],
                     preferred_element_type=F32)
        r_parts.append(ra[:, :MXU_DIM])
        i_parts.append(ra[:, MXU_DIM:])
    r = jax.nn.sigmoid(jnp.concatenate(r_parts, axis=-1) + ba_ref[...])
    gate_i = jax.nn.sigmoid(jnp.concatenate(i_parts, axis=-1) + bx_ref[...])
    neg_lam = -lam_ref[...]
    softplus = jnp.maximum(neg_lam, 0.0) + jnp.log1p(jnp.exp(-jnp.abs(neg_lam)))
    a = jnp.exp2((-RG_C * LOG2E * softplus) * r)
    y = 1.0 - a * a
    mult = y * lax.rsqrt(jnp.maximum(y, 1e-30))
    row = lax.broadcasted_iota(jnp.int32, (tile, WIDTH), 0)
    mult = jnp.where(jnp.logical_and(first, row == 0), 1.0, mult)
    b = xb * gate_i * mult

    srow = lax.broadcasted_iota(jnp.int32, (SUBLANES, WIDTH), 0)
    h = h_ref[...]
    hs = []
    for gi in range(tile // SUBLANES):
        av = a[gi * SUBLANES:(gi + 1) * SUBLANES, :]
        bv = b[gi * SUBLANES:(gi + 1) * SUBLANES, :]
        for d in (1, 2, 4):
            keep = srow >= d
            a_sh = jnp.where(keep, pltpu.roll(av, d, 0), 1.0)
            b_sh = jnp.where(keep, pltpu.roll(bv, d, 0), 0.0)
            bv = av * b_sh + bv
            av = av * a_sh
        hv = av * h + bv
        hs.append(hv)
        h = jnp.broadcast_to(hv[SUBLANES - 1:SUBLANES, :], (SUBLANES, WIDTH))
    h_ref[...] = h
    return (jnp.concatenate(hs, axis=0) * g_ref[...].astype(F32)).astype(BF16)


def _mix_merge_kernel(x_ref, au_ref, av_ref, bg_ref, bx_ref, oc_ref, ga_ref, gb_ref, gc_ref, g1_ref,
                      ln_ref, ws_ref, bs_ref, cw_ref, cb_ref, wax_ref, ba_ref, bxb_ref, lam_ref,
                      pa_ref, pb_ref, pc_ref, wo_ref, o_ref, xs_ref, h_ref):
    first = pl.program_id(1) == 0

    @pl.when(first)
    def _():
        xs_ref[0:SUBLANES, :] = jnp.zeros((SUBLANES, WIDTH), F32)
        h_ref[...] = jnp.zeros(h_ref.shape, F32)

    out_a = _gmlp(au_ref, av_ref, ln_ref, ws_ref, bs_ref)
    out_b = _rglru(bx_ref, bg_ref, cw_ref, cb_ref, wax_ref, ba_ref, bxb_ref, lam_ref, xs_ref, h_ref, first)
    merged = None
    for out, g_r, p_r in ((out_a, ga_ref, pa_ref), (out_b, gb_ref, pb_ref), (oc_ref[...], gc_ref, pc_ref)):
        term = g_r[...].astype(F32) * jnp.dot(out, p_r[...], preferred_element_type=F32)
        merged = term if merged is None else merged + term
    y = jnp.dot(merged.astype(BF16), wo_ref[...], preferred_element_type=F32)
    o_ref[...] = x_ref[...] + g1_ref[...] * y


def _mix_merge(x2d, z, out_c, ada3, a_ln, a_ws, bs_full, conv_w, conv_b, wax, ba, bx, lam,
               p_a, p_b, p_c, w_out, bsz, seq, tile):
    n, d = x2d.shape
    nt = seq // tile
    rows = lambda b, j: b * nt + j
    branch = lambda zb: pl.BlockSpec((tile, WIDTH), lambda b, j: (rows(b, j), zb))
    gate = lambda k: pl.BlockSpec((tile, d), lambda b, j: (rows(b, j), k))
    whole = lambda *shape: pl.BlockSpec(shape, lambda b, j: (0,) * len(shape))
    return pl.pallas_call(
        _mix_merge_kernel,
        out_shape=jax.ShapeDtypeStruct((n, d), F32),
        grid=(bsz, nt),
        in_specs=[
            pl.BlockSpec((tile, d), lambda b, j: (rows(b, j), 0)),
            branch(ZB_AU), branch(ZB_AV), branch(ZB_BG), branch(ZB_BX),
            pl.BlockSpec((tile, WIDTH), lambda b, j: (rows(b, j), 0)),
            gate(0), gate(1), gate(2),
            pl.BlockSpec((None, 1, d), lambda b, j: (b, 0, 2)),
            whole(1, WIDTH), whole(N_HEADS, A_CHUNK, A_CHUNK), whole(A_CHUNK, WIDTH),
            whole(B_CONV, WIDTH), whole(1, WIDTH), whole(WIDTH // MXU_DIM, MXU_DIM, 2 * MXU_DIM),
            whole(1, WIDTH), whole(1, WIDTH), whole(1, WIDTH),
            whole(WIDTH, d), whole(WIDTH, d), whole(WIDTH, d), whole(d, d),
        ],
        out_specs=pl.BlockSpec((tile, d), lambda b, j: (rows(b, j), 0)),
        scratch_shapes=[
            pltpu.VMEM((tile + SUBLANES, WIDTH), F32),
            pltpu.VMEM((SUBLANES, WIDTH), F32),
        ],
        input_output_aliases={0: 0},
        compiler_params=pltpu.CompilerParams(dimension_semantics=("arbitrary", "arbitrary"),
                                             vmem_limit_bytes=VMEM_LIMIT),
        name="mix_merge",
    )(x2d, z, z, z, z, out_c, z, z, z, ada3, a_ln, a_ws, bs_full, conv_w, conv_b, wax, ba, bx, lam,
      p_a, p_b, p_c, w_out)


FFN_HALO = BF16_ROWS
FFN_FBLOCK = MXU_DIM


def _ffn_kernel(x_ref, xh_ref, g_ref, sh_ref, sc_ref, g2_ref, wg_ref, wu_ref, cw_ref, cb_ref, wd_ref,
                fin_ref, o_ref, h_ref, *, tiles_per_seq, final):
    x = x_ref[...]
    xn = _rms_mod(x, g_ref[...], sh_ref[...], sc_ref[...]).astype(BF16)
    seq_start = pl.program_id(0) % tiles_per_seq == 0
    xh = _rms_mod(xh_ref[...], g_ref[...], sh_ref[...], sc_ref[...])
    xh = jnp.where(seq_start, 0.0, xh).astype(BF16)
    xe = jnp.concatenate([xh, xn], axis=0)
    for f in range(0, D_FF, FFN_FBLOCK):
        cols = slice(f, f + FFN_FBLOCK)
        ge = jnp.dot(xe, wg_ref[:, cols], preferred_element_type=F32)
        up = jnp.dot(xn, wu_ref[:, cols], preferred_element_type=F32)
        gc = cb_ref[:, cols] + cw_ref[FFN_CONV - 1:FFN_CONV, cols] * ge[FFN_HALO:, :]
        for k in range(FFN_CONV - 1):
            back = FFN_CONV - 1 - k
            gc = gc + cw_ref[k:k + 1, cols] * pltpu.roll(ge, back, 0)[FFN_HALO:, :]
        h_ref[:, cols] = (_gelu(gc) * up).astype(BF16)
    y = jnp.dot(h_ref[...], wd_ref[...], preferred_element_type=F32)
    xo = x + g2_ref[...] * y
    if final:
        xo = xo * lax.rsqrt(jnp.mean(xo * xo, axis=-1, keepdims=True) + EPS) * fin_ref[...]
    o_ref[...] = xo


def _ffn(x2d, norm_g, ada3, w_gate, w_up, conv_w, conv_b, w_down, final_g, seq, tm, final):
    n, d = x2d.shape
    tiles_per_seq = seq // tm
    halo_blocks = tm // FFN_HALO
    vec = pl.BlockSpec((1, d), lambda i: (0, 0))
    ada = lambda k: pl.BlockSpec((None, 1, d), lambda i: (i * tm // seq, 0, k))
    resident = dict(pipeline_mode=pl.Buffered(1))
    return pl.pallas_call(
        functools.partial(_ffn_kernel, tiles_per_seq=tiles_per_seq, final=final),
        out_shape=jax.ShapeDtypeStruct((n, d), F32),
        grid=(n // tm,),
        in_specs=[
            pl.BlockSpec((tm, d), lambda i: (i, 0)),
            pl.BlockSpec((FFN_HALO, d), lambda i: (jnp.maximum(i * halo_blocks - 1, 0), 0)),
            vec, ada(3), ada(4), ada(5),
            pl.BlockSpec((d, D_FF), lambda i: (0, 0), **resident),
            pl.BlockSpec((d, D_FF), lambda i: (0, 0), **resident),
            pl.BlockSpec((FFN_CONV, D_FF), lambda i: (0, 0)),
            pl.BlockSpec((1, D_FF), lambda i: (0, 0)),
            pl.BlockSpec((D_FF, d), lambda i: (0, 0), **resident),
            vec,
        ],
        out_specs=pl.BlockSpec((tm, d), lambda i: (i, 0)),
        scratch_shapes=[pltpu.VMEM((tm, D_FF), BF16)],
        compiler_params=pltpu.CompilerParams(dimension_semantics=("arbitrary",),
                                             vmem_limit_bytes=VMEM_LIMIT),
        name="ffn",
    )(x2d, x2d, norm_g, ada3, ada3, ada3, w_gate, w_up, conv_w, conv_b, w_down, final_g)


def _block_diag(w, per):
    depth, nb, r, c = w.shape
    w = w.reshape(depth, nb // per, per, r, c)
    eye = jnp.eye(per, dtype=w.dtype)
    return jnp.einsum("dbirc,ij->dbirjc", w, eye).reshape(depth, nb // per, per * r, per * c)


def kernel(x, c, w_ada, b_ada, norm1, w_in, a_ln, a_ws, a_bs, b_conv_w, b_conv_b, b_wa, b_ba, b_wx, b_bx,
           b_lam, rel_table, p_a, p_b, p_c, w_out, norm2, f_wgate, f_wup, f_conv_w, f_conv_b, f_wdown,
           final_norm):
    bsz, seq, d = x.shape
    depth = w_in.shape[0]
    n = bsz * seq

    per = MXU_DIM // HEAD_DIM
    w_in_p = jnp.concatenate(
        [w_in[..., _W_IN_OFFSET[m]:_W_IN_OFFSET[m] + _Z_WIDTH[m]] for m in _Z_ORDER], axis=-1).astype(BF16)
    wax = jnp.concatenate([_block_diag(b_wa, per), _block_diag(b_wx, per)], axis=-1).astype(BF16)
    a_ws_b = a_ws.astype(BF16)
    bs_full = jnp.repeat(jnp.swapaxes(a_bs, 1, 2), HEAD_DIM, axis=2)
    p_a_b, p_b_b, p_c_b, w_out_b = (w.astype(BF16) for w in (p_a, p_b, p_c, w_out))
    wg_b, wu_b, wd_b = (w.astype(BF16) for w in (f_wgate, f_wup, f_wdown))

    ada = _ada(c, w_ada, b_ada)
    bias = _bias_tables(rel_table)

    x2d = x.reshape(n, d)
    row = lambda v: v.reshape(1, -1)
    for l in range(depth):
        ada3 = ada[l].reshape(bsz, 1, 6 * d)
        z = _inproj(x2d, row(norm1[l]), ada3, w_in_p[l], seq, tm=1024)
        out_c = _mixc(z, bias, bsz, seq)
        x2d = _mix_merge(x2d, z, out_c, ada3, row(a_ln[l]), a_ws_b[l], bs_full[l], b_conv_w[l],
                         row(b_conv_b[l]), wax[l], row(b_ba[l]), row(b_bx[l]), row(b_lam[l]),
                         p_a_b[l], p_b_b[l], p_c_b[l], w_out_b[l], bsz, seq, tile=512)
        x2d = _ffn(x2d, row(norm2[l]), ada3, wg_b[l], wu_b[l], f_conv_w[l], row(f_conv_b[l]), wd_b[l],
                   row(final_norm), seq, tm=512, final=(l == depth - 1))
    return x2d.reshape(bsz, seq, d)
```
